```python
import math
import jax, jax.numpy as jnp
from jax import lax
import numpy as np

D_MODEL = 1024
BATCH = 8
SEQ = 2048
DEPTH = 4
DEC_BATCH = 32
DEC_SEQ = 4
PAST_LEN = 8192
PAGE_SIZE = 128

DIFF_HEAD_DIM = 64
DIFF_HEADS = D_MODEL // 256
DIFF_WIDTH = DIFF_HEADS * 2 * DIFF_HEAD_DIM
FOX_HEAD_DIM = 64
FOX_HEADS = D_MODEL // 128
FOX_WIDTH = FOX_HEADS * FOX_HEAD_DIM
LRU_WIDTH = D_MODEL // 2
LRU_BLOCKS = 8
LRU_BLOCK = LRU_WIDTH // LRU_BLOCKS
LRU_C = 8.0
CONV_WIDTH = 4
D_FF = ((8 * D_MODEL // 3 + 127) // 128) * 128
N_BRANCH = 3
Q_BLOCK = 128
ROPE_THETA = 10000.0
EPS = 1e-6
IN_SPLITS = (DIFF_WIDTH, DIFF_WIDTH, DIFF_WIDTH, FOX_WIDTH, FOX_WIDTH, FOX_WIDTH, FOX_HEADS, LRU_WIDTH, LRU_WIDTH, N_BRANCH * D_MODEL)
IN_COLS = sum(IN_SPLITS)

kernel_name = 'hybrid_diffattn_fox_rglru_macaron_step'


def _rms(x, g):
    xf = x.astype(jnp.float32)
    y = xf * lax.rsqrt(jnp.mean(xf * xf, axis=-1, keepdims=True) + EPS)
    return (y * g.astype(jnp.float32)).astype(x.dtype)


def _rope(x, pos):
    half = x.shape[-1] // 2
    inv = ROPE_THETA ** (-jnp.arange(half, dtype=jnp.float32) / half)
    ang = pos.astype(jnp.float32)[:, None] * inv[None, :]
    cos = jnp.cos(ang)[None, :, None, :]
    sin = jnp.sin(ang)[None, :, None, :]
    xf = x.astype(jnp.float32)
    x1, x2 = xf[..., :half], xf[..., half:]
    return jnp.concatenate([x1 * cos - x2 * sin, x1 * sin + x2 * cos], axis=-1).astype(x.dtype)


def _swiglu(x, w_in, w_out):
    g, u = jnp.split(x @ w_in, 2, axis=-1)
    return (jax.nn.silu(g) * u) @ w_out


def _query_blocks(fn, n_q):
    qb = Q_BLOCK if n_q % Q_BLOCK == 0 else n_q
    out = lax.map(lambda i: fn(i * qb, qb), jnp.arange(n_q // qb))
    _, b, _, h, e = out.shape
    return jnp.moveaxis(out, 0, 1).reshape(b, n_q, h, e)


def _causal_mask(start, qb, past_len, n_k):
    q_pos = past_len + start + jnp.arange(qb)
    return jnp.arange(n_k)[None, :] <= q_pos[:, None]


def _diff_attention(q, k, v, lam, past_len):
    n_q, n_k = q.shape[1], k.shape[1]
    scale = DIFF_HEAD_DIM ** -0.5
    vf = v.astype(jnp.float32)

    def block(start, qb):
        qs = lax.dynamic_slice_in_dim(q, start, qb, axis=1)
        logits = jnp.einsum('bqhcd,bkhcd->bchqk', qs, k).astype(jnp.float32) * scale
        mask = _causal_mask(start, qb, past_len, n_k)
        p = jax.nn.softmax(jnp.where(mask, logits, -jnp.inf), axis=-1)
        attn = p[:, 0] - lam * p[:, 1]
        return jnp.einsum('bhqk,bkhe->bqhe', attn, vf)

    return _query_blocks(block, n_q)


def _forgetting_attention(q, k, v, f_q, f_k, past_len):
    n_q, n_k = q.shape[1], k.shape[1]
    scale = FOX_HEAD_DIM ** -0.5
    vf = v.astype(jnp.float32)
    fk = jnp.swapaxes(f_k, 1, 2)[:, :, None, :]

    def block(start, qb):
        qs = lax.dynamic_slice_in_dim(q, start, qb, axis=1)
        fq = jnp.swapaxes(lax.dynamic_slice_in_dim(f_q, start, qb, axis=1), 1, 2)[..., None]
        logits = jnp.einsum('bqhd,bkhd->bhqk', qs, k).astype(jnp.float32) * scale + (fq - fk)
        mask = _causal_mask(start, qb, past_len, n_k)
        p = jax.nn.softmax(jnp.where(mask, logits, -jnp.inf), axis=-1)
        return jnp.einsum('bhqk,bkhd->bqhd', p, vf)

    return _query_blocks(block, n_q)


def _rg_lru(xb, h0, conv0, conv_w, conv_b, w_a, b_a, w_x, b_x, lam):
    n_t = xb.shape[1]
    xpad = jnp.concatenate([conv0.astype(xb.dtype), xb], axis=1)
    xc = sum((xpad[:, j:j + n_t] * conv_w[j] for j in range(CONV_WIDTH)), conv_b)
    xcb = xc.reshape(xc.shape[:2] + (LRU_BLOCKS, LRU_BLOCK))
    r = jax.nn.sigmoid(jnp.einsum('btnd,nde->btne', xcb, w_a).reshape(xc.shape) + b_a)
    i = jax.nn.sigmoid(jnp.einsum('btnd,nde->btne', xcb, w_x).reshape(xc.shape) + b_x)
    log_a = -LRU_C * r.astype(jnp.float32) * jax.nn.softplus(-lam.astype(jnp.float32))
    a = jnp.exp(log_a)
    u = jnp.sqrt(-jnp.expm1(2.0 * log_a)) * (i * xc).astype(jnp.float32)

    def step(h, au):
        a_t, u_t = au
        h = a_t * h + u_t
        return h, h

    h_last, hs = lax.scan(step, h0.astype(jnp.float32), (jnp.swapaxes(a, 0, 1), jnp.swapaxes(u, 0, 1)))
    return jnp.swapaxes(hs, 0, 1).astype(xb.dtype), h_last.astype(h0.dtype), xpad[:, n_t:]


def _layer(x, past_len, past_dk, past_dv, past_fk, past_fv, past_fl, h0, conv0, lp, lam_init):
    b, t, _ = x.shape
    dt = x.dtype
    pos = past_len + jnp.arange(t)
    x = x + 0.5 * _swiglu(_rms(x, lp['norm_ffn1']), lp['w_ffn1_in'], lp['w_ffn1_out'])
    xn = _rms(x, lp['norm_mix'])
    dq, dk, dv, fq, fk, fv, ff, lx, lg, gates = jnp.split(
        xn @ lp['w_in'], np.cumsum(IN_SPLITS)[:-1].tolist(), axis=-1)

    dq = _rope(_rms(dq.reshape(b, t, 2 * DIFF_HEADS, DIFF_HEAD_DIM), lp['diff_q_norm']), pos)
    dk = _rope(_rms(dk.reshape(b, t, 2 * DIFF_HEADS, DIFF_HEAD_DIM), lp['diff_k_norm']), pos)
    dk_row = dk.reshape(b, t, DIFF_HEADS, 2 * DIFF_HEAD_DIM)
    dv_row = dv.reshape(b, t, DIFF_HEADS, 2 * DIFF_HEAD_DIM)
    dk_all = jnp.concatenate([past_dk, dk_row], axis=1).reshape(b, -1, DIFF_HEADS, 2, DIFF_HEAD_DIM)
    dv_all = jnp.concatenate([past_dv, dv_row], axis=1)
    f32 = jnp.float32
    lam = (jnp.exp(jnp.sum(lp['lq1'].astype(f32) * lp['lk1'].astype(f32)))
           - jnp.exp(jnp.sum(lp['lq2'].astype(f32) * lp['lk2'].astype(f32))) + lam_init)
    d_out = _diff_attention(dq.reshape(b, t, DIFF_HEADS, 2, DIFF_HEAD_DIM), dk_all, dv_all, lam, past_len)
    d_out = (_rms(d_out, lp['diff_subln']) * (1.0 - lam_init)).reshape(b, t, DIFF_WIDTH).astype(dt)

    fq = _rms(fq.reshape(b, t, FOX_HEADS, FOX_HEAD_DIM), lp['fox_q_norm'])
    fk = _rms(fk.reshape(b, t, FOX_HEADS, FOX_HEAD_DIM), lp['fox_k_norm'])
    fv = fv.reshape(b, t, FOX_HEADS, FOX_HEAD_DIM)
    lf = jax.nn.log_sigmoid(ff.astype(f32) + lp['b_forget'].astype(f32))
    past_lf = past_fl.astype(f32)
    c_past = jnp.cumsum(past_lf, axis=1) - jnp.sum(past_lf, axis=1, keepdims=True)
    f_new = jnp.cumsum(lf, axis=1)
    f_k = jnp.concatenate([c_past, f_new], axis=1)
    f_out = _forgetting_attention(fq, jnp.concatenate([past_fk, fk], axis=1),
                                  jnp.concatenate([past_fv, fv], axis=1), f_new, f_k, past_len)
    f_out = f_out.reshape(b, t, FOX_WIDTH).astype(dt)

    hs, h_last, conv_last = _rg_lru(lx, h0, conv0, lp['conv_w'], lp['conv_b'], lp['lru_w_a'],
                                    lp['lru_b_a'], lp['lru_w_x'], lp['lru_b_x'], lp['lru_lambda'])
    l_out = hs * jax.nn.gelu(lg)

    g_d, g_f, g_l = jnp.split(jax.nn.sigmoid(gates), N_BRANCH, axis=-1)
    merged = (g_d * (d_out @ lp['w_diff_out']) + g_f * (f_out @ lp['w_fox_out'])
              + g_l * (l_out @ lp['w_lru_out']))
    x = x + merged @ lp['w_o']
    x = x + 0.5 * _swiglu(_rms(x, lp['norm_ffn2']), lp['w_ffn2_in'], lp['w_ffn2_out'])
    return x, (dk_row, dv_row, fk, fv, lf.astype(dt), h_last, conv_last)


def setup_inputs(seed: int = 0) -> dict:
    key = jax.random.key(seed)
    keys = jax.random.split(key, 64)
    counter = iter(range(64))
    f32 = jnp.float32

    def nk():
        return keys[next(counter)]

    def normal(shape, scale=1.0):
        return jax.random.normal(nk(), shape, f32) * scale

    def gain(shape):
        return 1.0 + 0.01 * normal(shape)

    n_pages = PAST_LEN // PAGE_SIZE
    n_used = DEC_BATCH * n_pages
    n_pool = n_used + max(1, n_used // 4)
    page_table = jax.random.permutation(nk(), n_pool)[:n_used].reshape(DEC_BATCH, n_pages).astype(jnp.int32)

    u = jax.random.uniform(nk(), (DEPTH, LRU_WIDTH), f32, 0.9, 0.999)
    s = u ** (1.0 / LRU_C)
    lru_lambda = jnp.log(s) - jnp.log1p(-s)

    return {
        'x_prompt': normal((BATCH, SEQ, D_MODEL)),
        'x_sample': normal((DEC_BATCH, DEC_SEQ, D_MODEL)),
        'cache_diff_k': normal((DEPTH, n_pool, PAGE_SIZE, DIFF_HEADS, 2 * DIFF_HEAD_DIM)),
        'cache_diff_v': normal((DEPTH, n_pool, PAGE_SIZE, DIFF_HEADS, 2 * DIFF_HEAD_DIM)),
        'cache_fox_k': normal((DEPTH, n_pool, PAGE_SIZE, FOX_HEADS, FOX_HEAD_DIM)),
        'cache_fox_v': normal((DEPTH, n_pool, PAGE_SIZE, FOX_HEADS, FOX_HEAD_DIM)),
        'cache_fox_logf': jax.nn.log_sigmoid(3.0 + normal((DEPTH, n_pool, PAGE_SIZE, FOX_HEADS), 0.5)),
        'state_lru_h': normal((DEPTH, DEC_BATCH, LRU_WIDTH), 0.5),
        'state_conv': normal((DEPTH, DEC_BATCH, CONV_WIDTH - 1, LRU_WIDTH)),
        'page_table': page_table,
        'norm_ffn1': gain((DEPTH, D_MODEL)),
        'w_ffn1_in': normal((DEPTH, D_MODEL, 2 * D_FF), D_MODEL ** -0.5),
        'w_ffn1_out': normal((DEPTH, D_FF, D_MODEL), D_FF ** -0.5),
        'norm_mix': gain((DEPTH, D_MODEL)),
        'w_in': normal((DEPTH, D_MODEL, IN_COLS), D_MODEL ** -0.5),
        'b_forget': jax.random.uniform(nk(), (DEPTH, FOX_HEADS), f32, 1.5, 4.5),
        'diff_q_norm': gain((DEPTH, DIFF_HEAD_DIM)),
        'diff_k_norm': gain((DEPTH, DIFF_HEAD_DIM)),
        'diff_lambda_q1': normal((DEPTH, DIFF_HEAD_DIM), 0.1),
        'diff_lambda_k1': normal((DEPTH, DIFF_HEAD_DIM), 0.1),
        'diff_lambda_q2': normal((DEPTH, DIFF_HEAD_DIM), 0.1),
        'diff_lambda_k2': normal((DEPTH, DIFF_HEAD_DIM), 0.1),
        'diff_subln': gain((DEPTH, 2 * DIFF_HEAD_DIM)),
        'fox_q_norm': gain((DEPTH, FOX_HEAD_DIM)),
        'fox_k_norm': gain((DEPTH, FOX_HEAD_DIM)),
        'conv_w': normal((DEPTH, CONV_WIDTH, LRU_WIDTH), CONV_WIDTH ** -0.5),
        'conv_b': normal((DEPTH, LRU_WIDTH), 0.01),
        'lru_w_a': normal((DEPTH, LRU_BLOCKS, LRU_BLOCK, LRU_BLOCK), LRU_BLOCK ** -0.5),
        'lru_b_a': normal((DEPTH, LRU_WIDTH), 0.01),
        'lru_w_x': normal((DEPTH, LRU_BLOCKS, LRU_BLOCK, LRU_BLOCK), LRU_BLOCK ** -0.5),
        'lru_b_x': normal((DEPTH, LRU_WIDTH), 0.01),
        'lru_lambda': lru_lambda,
        'w_diff_out': normal((DEPTH, DIFF_WIDTH, D_MODEL), DIFF_WIDTH ** -0.5),
        'w_fox_out': normal((DEPTH, FOX_WIDTH, D_MODEL), FOX_WIDTH ** -0.5),
        'w_lru_out': normal((DEPTH, LRU_WIDTH, D_MODEL), LRU_WIDTH ** -0.5),
        'w_o': normal((DEPTH, D_MODEL, D_MODEL), D_MODEL ** -0.5),
        'norm_ffn2': gain((DEPTH, D_MODEL)),
        'w_ffn2_in': normal((DEPTH, D_MODEL, 2 * D_FF), D_MODEL ** -0.5),
        'w_ffn2_out': normal((DEPTH, D_FF, D_MODEL), D_FF ** -0.5),
    }


def reference(x_prompt, x_sample, cache_diff_k, cache_diff_v, cache_fox_k, cache_fox_v, cache_fox_logf,
              state_lru_h, state_conv, page_table, norm_ffn1, w_ffn1_in, w_ffn1_out, norm_mix, w_in,
              b_forget, diff_q_norm, diff_k_norm, diff_lambda_q1, diff_lambda_k1, diff_lambda_q2,
              diff_lambda_k2, diff_subln, fox_q_norm, fox_k_norm, conv_w, conv_b, lru_w_a, lru_b_a,
              lru_w_x, lru_b_x, lru_lambda, w_diff_out, w_fox_out, w_lru_out, w_o, norm_ffn2,
              w_ffn2_in, w_ffn2_out):
    past_len = page_table.shape[1] * cache_diff_k.shape[2]
    bp = x_prompt.shape[0]
    dt = x_prompt.dtype

    def gather(cache, l):
        c = cache[l, page_table]
        return c.reshape((c.shape[0], -1) + c.shape[3:])

    def empty(tail):
        return jnp.zeros((bp, 0) + tail, dt)

    yp, ys = x_prompt, x_sample
    p_rows, s_rows = [], []
    for l in range(DEPTH):
        lam_init = 0.8 - 0.6 * math.exp(-0.3 * l)
        lp = {
            'norm_ffn1': norm_ffn1[l], 'w_ffn1_in': w_ffn1_in[l], 'w_ffn1_out': w_ffn1_out[l],
            'norm_mix': norm_mix[l], 'w_in': w_in[l], 'b_forget': b_forget[l],
            'diff_q_norm': diff_q_norm[l], 'diff_k_norm': diff_k_norm[l],
            'lq1': diff_lambda_q1[l], 'lk1': diff_lambda_k1[l], 'lq2': diff_lambda_q2[l], 'lk2': diff_lambda_k2[l],
            'diff_subln': diff_subln[l], 'fox_q_norm': fox_q_norm[l], 'fox_k_norm': fox_k_norm[l],
            'conv_w': conv_w[l], 'conv_b': conv_b[l], 'lru_w_a': lru_w_a[l], 'lru_b_a': lru_b_a[l],
            'lru_w_x': lru_w_x[l], 'lru_b_x': lru_b_x[l], 'lru_lambda': lru_lambda[l],
            'w_diff_out': w_diff_out[l], 'w_fox_out': w_fox_out[l], 'w_lru_out': w_lru_out[l], 'w_o': w_o[l],
            'norm_ffn2': norm_ffn2[l], 'w_ffn2_in': w_ffn2_in[l], 'w_ffn2_out': w_ffn2_out[l],
        }
        yp, pr = _layer(yp, 0,
                        empty((DIFF_HEADS, 2 * DIFF_HEAD_DIM)), empty((DIFF_HEADS, 2 * DIFF_HEAD_DIM)),
                        empty((FOX_HEADS, FOX_HEAD_DIM)), empty((FOX_HEADS, FOX_HEAD_DIM)), empty((FOX_HEADS,)),
                        jnp.zeros((bp, LRU_WIDTH), dt), jnp.zeros((bp, CONV_WIDTH - 1, LRU_WIDTH), dt),
                        lp, lam_init)
        ys, sr = _layer(ys, past_len,
                        gather(cache_diff_k, l), gather(cache_diff_v, l),
                        gather(cache_fox_k, l), gather(cache_fox_v, l), gather(cache_fox_logf, l),
                        state_lru_h[l], state_conv[l], lp, lam_init)
        p_rows.append(pr)
        s_rows.append(sr)

    def stack(rows, i):
        return jnp.stack([r[i] for r in rows])

    p_dk, p_dv, p_fk, p_fv, p_fl, p_h, p_conv = [stack(p_rows, i) for i in range(7)]
    s_dk, s_dv, s_fk, s_fv, s_fl, s_h, s_conv = [stack(s_rows, i) for i in range(7)]
    return (yp, ys, p_dk, p_dv, p_fk, p_fv, p_fl, p_h, p_conv, s_dk, s_dv, s_fk, s_fv, s_fl, s_h, s_conv)
```

```python
import functools
import math

import jax
import jax.numpy as jnp
from jax import lax
from jax.experimental import pallas as pl
from jax.experimental.pallas import tpu as pltpu

F32 = jnp.float32
BF16 = jnp.bfloat16

D_MODEL = 1024
DIFF_HEADS = 4
DIFF_HEAD_DIM = 64
FOX_HEADS = 8
FOX_HEAD_DIM = 64
WIDTH = 512
HEAD_GROUPS = WIDTH // 128
LRU_C = 8.0
CONV_WIDTH = 4
D_FF = 2816
N_BRANCH = 3
ROPE_THETA = 10000.0
EPS = 1e-6
QKV_COLS = 8 * WIDTH + 128

V7X_VMEM_LIMIT = 56 * 1024 * 1024
LANES = 128


def _params(semantics):
    return pltpu.CompilerParams(dimension_semantics=semantics, vmem_limit_bytes=V7X_VMEM_LIMIT)


def _resident(shape, index_map):
    return pl.BlockSpec(shape, index_map, pipeline_mode=pl.Buffered(1))


def _rms(x, g):
    return x * lax.rsqrt(jnp.mean(x * x, axis=-1, keepdims=True) + EPS) * g


def _dot(a, b):
    return jnp.dot(a, b, preferred_element_type=F32)


def _dot_nt(a, b):
    return lax.dot_general(a, b, (((1,), (1,)), ((), ())), preferred_element_type=F32)


def _softplus(x):
    return jnp.maximum(x, 0.0) + jnp.log1p(jnp.exp(-jnp.abs(x)))


def _gelu_tanh(x):
    return 0.5 * x * (1.0 + jnp.tanh(math.sqrt(2.0 / math.pi) * (x + 0.044715 * (x * x * x))))


def _ffn_body(x_ref, g_ref, wg_ref, wu_ref, wo_ref, o_ref):
    x = x_ref[...]
    xb = _rms(x, g_ref[...]).astype(BF16)
    g = _dot(xb, wg_ref[...])
    u = _dot(xb, wu_ref[...])
    act = (g * jax.nn.sigmoid(g) * u).astype(BF16)
    o_ref[...] = x + 0.5 * _dot(act, wo_ref[...])


def _ffn(x, gain, w_in, w_out, l, tm):
    m = x.shape[0]
    return pl.pallas_call(
        _ffn_body,
        grid=(m // tm,),
        in_specs=[
            pl.BlockSpec((tm, D_MODEL), lambda i: (i, 0)),
            _resident((None, 1, D_MODEL), lambda i: (l, 0, 0)),
            _resident((None, D_MODEL, D_FF), lambda i: (l, 0, 0)),
            _resident((None, D_MODEL, D_FF), lambda i: (l, 0, 1)),
            _resident((None, D_FF, D_MODEL), lambda i: (l, 0, 0)),
        ],
        out_specs=pl.BlockSpec((tm, D_MODEL), lambda i: (i, 0)),
        out_shape=jax.ShapeDtypeStruct((m, D_MODEL), F32),
        compiler_params=_params(("parallel",)),
        name="ffn",
    )(x, gain, w_in, w_in, w_out)


def _proj_body(x_ref, g_ref, w_ref, bd_ref, cos_ref, sin_ref, gdq_ref, gdk_ref, gfq_ref, gfk_ref, bf_ref,
               dq_o, dk_o, dkb_o, dv_o, dvb_o, fq_o, fk_o, fkb_o, fv_o, fvb_o, lf_o, lx_o, lg_o):
    tm = x_ref.shape[0]
    xb = _rms(x_ref[...], g_ref[...]).astype(BF16)
    y = _dot(xb, w_ref[...])
    bd = bd_ref[...]

    def head_rms(t, g):
        ss = _dot((t * t).astype(BF16), bd)
        return t * lax.rsqrt(ss * (1.0 / DIFF_HEAD_DIM) + EPS) * g

    lane = lax.broadcasted_iota(jnp.int32, (tm, WIDTH), 1)
    first_half = (lane & (DIFF_HEAD_DIM // 2)) == 0
    cos = cos_ref[...]
    sin = sin_ref[...]

    def rope(t):
        partner = jnp.where(first_half, pltpu.roll(t, WIDTH - DIFF_HEAD_DIM // 2, 1),
                            pltpu.roll(t, DIFF_HEAD_DIM // 2, 1))
        return t * cos + partner * sin

    scale = DIFF_HEAD_DIM ** -0.5
    dq = rope(head_rms(y[:, 0 * WIDTH:1 * WIDTH], gdq_ref[...]))
    dk = rope(head_rms(y[:, 1 * WIDTH:2 * WIDTH], gdk_ref[...]))
    dv = y[:, 2 * WIDTH:3 * WIDTH]
    fq = head_rms(y[:, 3 * WIDTH:4 * WIDTH], gfq_ref[...])
    fk = head_rms(y[:, 4 * WIDTH:5 * WIDTH], gfk_ref[...])
    fv = y[:, 5 * WIDTH:6 * WIDTH]
    z = y[:, 8 * WIDTH:8 * WIDTH + LANES] + bf_ref[...]
    lf = jnp.minimum(z, 0.0) - jnp.log1p(jnp.exp(-jnp.abs(z)))

    dq_o[...] = (dq * scale).astype(BF16)
    dk_o[...] = dk
    dkb_o[...] = dk.astype(BF16)
    dv_o[...] = dv
    dvb_o[...] = dv.astype(BF16)
    fq_o[...] = (fq * scale).astype(BF16)
    fk_o[...] = fk
    fkb_o[...] = fk.astype(BF16)
    fv_o[...] = fv
    fvb_o[...] = fv.astype(BF16)
    lf_o[...] = lf[:, :FOX_HEADS]
    lx_o[...] = y[:, 6 * WIDTH:7 * WIDTH]
    lg_o[...] = y[:, 7 * WIDTH:8 * WIDTH]


def _proj(x, W, l, tm, cos, sin, n_pos_blocks):
    m = x.shape[0]
    row = lambda width: pl.BlockSpec((tm, width), lambda i: (i, 0))
    vec = lambda width: _resident((None, 1, width), lambda i: (l, 0, 0))
    pos = pl.BlockSpec((tm, WIDTH), lambda i: (i % n_pos_blocks, 0))
    f32w = jax.ShapeDtypeStruct((m, WIDTH), F32)
    b16w = jax.ShapeDtypeStruct((m, WIDTH), BF16)
    return pl.pallas_call(
        _proj_body,
        grid=(m // tm,),
        in_specs=[
            row(D_MODEL), vec(D_MODEL),
            _resident((None, D_MODEL, QKV_COLS), lambda i: (l, 0, 0)),
            _resident((WIDTH, WIDTH), lambda i: (0, 0)),
            pos, pos, vec(WIDTH), vec(WIDTH), vec(WIDTH), vec(WIDTH), vec(LANES),
        ],
        out_specs=[row(WIDTH)] * 10 + [row(FOX_HEADS), row(WIDTH), row(WIDTH)],
        out_shape=[b16w, f32w, b16w, f32w, b16w, b16w, f32w, b16w, f32w, b16w,
                   jax.ShapeDtypeStruct((m, FOX_HEADS), F32), f32w, f32w],
        compiler_params=_params(("parallel",)),
        name="proj",
    )(x, W["norm_mix"], W["w_qkv"], W["head_sum"], cos, sin,
      W["g_dq"], W["g_dk"], W["g_fq"], W["g_fk"], W["b_forget"])


def _merge_body(x_ref, g_ref, wg_ref, d_ref, f_ref, r_ref, wd_ref, wf_ref, wl_ref, wo_ref, o_ref):
    x = x_ref[...]
    xb = _rms(x, g_ref[...]).astype(BF16)
    gates = jax.nn.sigmoid(_dot(xb, wg_ref[...]))
    merged = (gates[:, 0:D_MODEL] * _dot(d_ref[...], wd_ref[...])
              + gates[:, D_MODEL:2 * D_MODEL] * _dot(f_ref[...], wf_ref[...])
              + gates[:, 2 * D_MODEL:] * _dot(r_ref[...], wl_ref[...]))
    o_ref[...] = x + _dot(merged.astype(BF16), wo_ref[...])


def _merge(x, d_out, f_out, l_out, W, l, tm):
    m = x.shape[0]
    row = lambda width: pl.BlockSpec((tm, width), lambda i: (i, 0))
    mat = lambda r, c: _resident((None, r, c), lambda i: (l, 0, 0))
    return pl.pallas_call(
        _merge_body,
        grid=(m // tm,),
        in_specs=[row(D_MODEL), mat(1, D_MODEL), mat(D_MODEL, N_BRANCH * D_MODEL),
                  row(WIDTH), row(WIDTH), row(WIDTH),
                  mat(WIDTH, D_MODEL), mat(WIDTH, D_MODEL), mat(WIDTH, D_MODEL), mat(D_MODEL, D_MODEL)],
        out_specs=row(D_MODEL),
        out_shape=jax.ShapeDtypeStruct((m, D_MODEL), F32),
        compiler_params=_params(("parallel",)),
        name="merge",
    )(x, W["norm_mix"], W["w_gates"], d_out, f_out, l_out,
      W["w_diff_out"], W["w_fox_out"], W["w_lru_out"], W["w_o"])


def _stack_halves(qg):
    lane = lax.broadcasted_iota(jnp.int32, qg.shape, 1)
    zero = jnp.zeros_like(qg)
    return jnp.concatenate([jnp.where(lane < 64, qg, zero), jnp.where(lane >= 64, qg, zero)], axis=0)


def _flash_group(qq, k_ref, v_ref, grp, n_full, t, bias_fn):
    rows = 2 * t
    row = lax.broadcasted_iota(jnp.int32, (rows, t), 0)
    col = lax.broadcasted_iota(jnp.int32, (rows, t), 1)
    causal = col <= jnp.where(row >= t, row - t, row)
    lanes = slice(grp * 128, (grp + 1) * 128)

    def step(j, carry, masked):
        m, l, acc = carry
        start = pl.multiple_of(j * t, t)
        s = _dot_nt(qq, k_ref[pl.ds(start, t), lanes])
        if bias_fn is not None:
            s = bias_fn(s, j)
        if masked:
            s = jnp.where(causal, s, -jnp.inf)
        m_new = jnp.maximum(m, jnp.max(s, axis=-1, keepdims=True))
        alpha = jnp.exp(m - m_new)
        p = jnp.exp(s - m_new)
        l = alpha * l + jnp.sum(p, axis=-1, keepdims=True)
        acc = alpha * acc + _dot(p.astype(BF16), v_ref[pl.ds(start, t), lanes])
        return m_new, l, acc

    init = (jnp.full((rows, 1), -jnp.inf, F32), jnp.zeros((rows, 1), F32), jnp.zeros((rows, 128), F32))
    carry = lax.fori_loop(0, n_full, lambda j, c: step(j, c, False), init)
    _, l, acc = step(n_full, carry, True)
    return l, acc


def _diff_lambda(lv_ref, lam_init):
    lv = lv_ref[...]
    s1 = jnp.sum(lv[0:1] * lv[1:2], axis=-1, keepdims=True)
    s2 = jnp.sum(lv[2:3] * lv[3:4], axis=-1, keepdims=True)
    return jnp.exp(s1) - jnp.exp(s2) + lam_init


def _diff_finish(l, acc, lam, gain, lam_init, t):
    o = acc[:t] / l[:t] - lam * (acc[t:] / l[t:])
    return _rms(o, gain) * (1.0 - lam_init)


def _pdiff_body(lv_ref, g_ref, q_ref, k_ref, v_ref, o_ref, *, t, lam_init):
    i = pl.program_id(1)
    lam = _diff_lambda(lv_ref, lam_init)
    for h in range(DIFF_HEADS):
        qq = _stack_halves(q_ref[:, h * 128:(h + 1) * 128])
        l, acc = _flash_group(qq, k_ref, v_ref, h, i, t, None)
        o_ref[:, h * 128:(h + 1) * 128] = _diff_finish(l, acc, lam, g_ref[...], lam_init, t).astype(BF16)


def _prompt_diff(q, k, v, W, l, lam_init, t):
    b, n_t, _ = q.shape
    seq = pl.BlockSpec((None, n_t, WIDTH), lambda bi, i: (bi, 0, 0))
    tile = pl.BlockSpec((None, t, WIDTH), lambda bi, i: (bi, i, 0))
    return pl.pallas_call(
        functools.partial(_pdiff_body, t=t, lam_init=lam_init),
        grid=(b, n_t // t),
        in_specs=[pl.BlockSpec((None, 4, DIFF_HEAD_DIM), lambda bi, i: (l, 0, 0)),
                  pl.BlockSpec((None, 1, 128), lambda bi, i: (l, 0, 0)),
                  tile, seq, seq],
        out_specs=tile,
        out_shape=jax.ShapeDtypeStruct((b, n_t, WIDTH), BF16),
        compiler_params=_params(("parallel", "arbitrary")),
        name="prompt_diff_attn",
    )(W["diff_lambda"], W["g_subln"], q, k, v)


def _pfox_body(q_ref, k_ref, v_ref, fq_ref, fk_ref, o_ref, *, t):
    i = pl.program_id(1)
    lane = lax.broadcasted_iota(jnp.int32, (t, 128), 1)
    for grp in range(HEAD_GROUPS):
        qq = _stack_halves(q_ref[:, grp * 128:(grp + 1) * 128])
        fq0 = fq_ref[:, 2 * grp:2 * grp + 1]
        fq1 = fq_ref[:, 2 * grp + 1:2 * grp + 2]

        def bias(s, j, grp=grp, fq0=fq0, fq1=fq1):
            fk0 = fk_ref[2 * grp, pl.ds(j, 1), :]
            fk1 = fk_ref[2 * grp + 1, pl.ds(j, 1), :]
            return jnp.concatenate([s[:t] + (fq0 - fk0), s[t:] + (fq1 - fk1)], axis=0)

        l, acc = _flash_group(qq, k_ref, v_ref, grp, i, t, bias)
        o_ref[:, grp * 128:(grp + 1) * 128] = jnp.where(lane < 64, acc[:t] / l[:t], acc[t:] / l[t:]).astype(BF16)


def _prompt_fox(q, k, v, f_col, f_row, t):
    b, n_t, _ = q.shape
    seq = pl.BlockSpec((None, n_t, WIDTH), lambda bi, i: (bi, 0, 0))
    tile = pl.BlockSpec((None, t, WIDTH), lambda bi, i: (bi, i, 0))
    return pl.pallas_call(
        functools.partial(_pfox_body, t=t),
        grid=(b, n_t // t),
        in_specs=[tile, seq, seq,
                  pl.BlockSpec((None, t, FOX_HEADS), lambda bi, i: (bi, i, 0)),
                  pl.BlockSpec((None, FOX_HEADS, n_t // t, t), lambda bi, i: (bi, 0, 0, 0))],
        out_specs=tile,
        out_shape=jax.ShapeDtypeStruct((b, n_t, WIDTH), BF16),
        compiler_params=_params(("parallel", "arbitrary")),
        name="prompt_fox_attn",
    )(q, k, v, f_col, f_row.reshape(b, FOX_HEADS, n_t // t, t))


def _lane_cumsum(x):
    n = x.shape[-1]
    lane = lax.broadcasted_iota(jnp.int32, x.shape, x.ndim - 1)
    s = 1
    while s < n:
        x = x + jnp.where(lane >= s, pltpu.roll(x, s, x.ndim - 1), 0.0)
        s *= 2
    return x


def _cumsum_body(x_ref, o_ref):
    o_ref[...] = _lane_cumsum(x_ref[...])


def _prompt_cum_forget(lf_rows):
    b, h, n_t = lf_rows.shape
    spec = pl.BlockSpec((None, h, n_t), lambda bi: (bi, 0, 0))
    return pl.pallas_call(
        _cumsum_body, grid=(b,), in_specs=[spec], out_specs=spec,
        out_shape=jax.ShapeDtypeStruct((b, h, n_t), F32),
        compiler_params=_params(("parallel",)), name="cum_forget",
    )(lf_rows)


def _lru_gates(xc, wax_ref, ba_ref, bx_ref, lam_ref):
    ga = _dot(xc.astype(BF16), wax_ref[...])
    r = jax.nn.sigmoid(ga[:, :WIDTH] + ba_ref[...])
    i = jax.nn.sigmoid(ga[:, WIDTH:] + bx_ref[...])
    log_a = -LRU_C * r * _softplus(-lam_ref[...])
    a = jnp.exp(log_a)
    u = jnp.sqrt(-jnp.tanh(log_a) * (a * a + 1.0)) * (i * xc)
    return a, u


def _plru_body(lx_ref, lg_ref, cw_ref, cb_ref, wax_ref, ba_ref, bx_ref, lam_ref, lo_ref, hl_ref,
               xs_ref, h_ref, *, tt):
    @pl.when(pl.program_id(1) == 0)
    def _():
        xs_ref[0:8, :] = jnp.zeros((8, WIDTH), F32)
        h_ref[...] = jnp.zeros((8, WIDTH), F32)

    x = lx_ref[...]
    xs_ref[8:8 + tt, :] = x
    cw = cw_ref[...]
    xc = cb_ref[...] + xs_ref[5:5 + tt, :] * cw[0:1]
    xc = xc + xs_ref[6:6 + tt, :] * cw[1:2]
    xc = xc + xs_ref[7:7 + tt, :] * cw[2:3]
    xc = xc + x * cw[3:4]
    xs_ref[0:8, :] = x[tt - 8:tt]

    a, u = _lru_gates(xc, wax_ref, ba_ref, bx_ref, lam_ref)
    row = lax.broadcasted_iota(jnp.int32, (tt, WIDTH), 0)
    s = 1
    while s < tt:
        keep = row >= s
        u = a * jnp.where(keep, pltpu.roll(u, s, 0), 0.0) + u
        a = a * jnp.where(keep, pltpu.roll(a, s, 0), 1.0)
        s *= 2
    hs = a * h_ref[0:1, :] + u
    h_ref[0:1, :] = hs[tt - 1:tt]
    hl_ref[...] = hs[tt - 1:tt]
    lo_ref[...] = (hs * _gelu_tanh(lg_ref[...])).astype(BF16)


def _prompt_lru(lx, lg, W, l, tt):
    b, n_t, _ = lx.shape
    tile = pl.BlockSpec((None, tt, WIDTH), lambda bi, i: (bi, i, 0))
    mat = lambda r, c: _resident((None, r, c), lambda bi, i: (l, 0, 0))
    return pl.pallas_call(
        functools.partial(_plru_body, tt=tt),
        grid=(b, n_t // tt),
        in_specs=[tile, tile, mat(CONV_WIDTH, WIDTH), mat(1, WIDTH), mat(WIDTH, 2 * WIDTH),
                  mat(1, WIDTH), mat(1, WIDTH), mat(1, WIDTH)],
        out_specs=[tile, pl.BlockSpec((None, 1, WIDTH), lambda bi, i: (bi, 0, 0))],
        out_shape=[jax.ShapeDtypeStruct((b, n_t, WIDTH), BF16), jax.ShapeDtypeStruct((b, 1, WIDTH), F32)],
        scratch_shapes=[pltpu.VMEM((tt + 8, WIDTH), F32), pltpu.VMEM((8, WIDTH), F32)],
        compiler_params=_params(("parallel", "arbitrary")),
        name="prompt_lru",
    )(lx, lg, W["conv_w"], W["conv_b"], W["lru_w_ax"], W["lru_b_a"], W["lru_b_x"], W["lru_lambda"])


def _slru_body(lx_ref, lg_ref, c0_ref, h0_ref, cw_ref, cb_ref, wax_ref, ba_ref, bx_ref, lam_ref,
               lo_ref, hl_ref, *, n_t, nb):
    cw = cw_ref[...]
    xs = [c0_ref[j] for j in range(CONV_WIDTH - 1)] + [lx_ref[t] for t in range(n_t)]
    xcs = []
    for t in range(n_t):
        xc = cb_ref[...] + xs[t] * cw[0:1]
        for j in range(1, CONV_WIDTH):
            xc = xc + xs[t + j] * cw[j:j + 1]
        xcs.append(xc)
    a, u = _lru_gates(jnp.concatenate(xcs, axis=0), wax_ref, ba_ref, bx_ref, lam_ref)
    h = h0_ref[...]
    for t in range(n_t):
        h = a[t * nb:(t + 1) * nb] * h + u[t * nb:(t + 1) * nb]
        lo_ref[t] = (h * _gelu_tanh(lg_ref[t])).astype(BF16)
    hl_ref[...] = h


def _sample_lru(lx_t, lg_t, conv0_t, h0, W, l):
    n_t, nb, _ = lx_t.shape
    full = lambda *shape: pl.BlockSpec(shape, lambda i: (0,) * len(shape))
    mat = lambda r, c: pl.BlockSpec((None, r, c), lambda i: (l, 0, 0))
    return pl.pallas_call(
        functools.partial(_slru_body, n_t=n_t, nb=nb),
        grid=(1,),
        in_specs=[full(n_t, nb, WIDTH), full(n_t, nb, WIDTH), full(CONV_WIDTH - 1, nb, WIDTH), full(nb, WIDTH),
                  mat(CONV_WIDTH, WIDTH), mat(1, WIDTH), mat(WIDTH, 2 * WIDTH),
                  mat(1, WIDTH), mat(1, WIDTH), mat(1, WIDTH)],
        out_specs=[full(n_t, nb, WIDTH), full(nb, WIDTH)],
        out_shape=[jax.ShapeDtypeStruct((n_t, nb, WIDTH), BF16), jax.ShapeDtypeStruct((nb, WIDTH), F32)],
        compiler_params=_params(("arbitrary",)),
        name="sample_lru",
    )(lx_t, lg_t, conv0_t, h0, W["conv_w"], W["conv_b"], W["lru_w_ax"], W["lru_b_a"], W["lru_b_x"],
      W["lru_lambda"])


PAGES_PER_STEP = 8


def _online_update(m_ref, l_ref, acc_ref, h, s, v, width):
    m_old = m_ref[h][:, 0:1]
    l_old = l_ref[h][:, 0:1]
    m_new = jnp.maximum(m_old, jnp.max(s, axis=-1, keepdims=True))
    alpha = jnp.exp(m_old - m_new)
    p = jnp.exp(s - m_new)
    l_new = alpha * l_old + jnp.sum(p, axis=-1, keepdims=True)
    acc_ref[h] = alpha * acc_ref[h] + _dot(p, v)
    m_ref[h] = jnp.broadcast_to(m_new, (8, LANES))
    l_ref[h] = jnp.broadcast_to(l_new, (8, LANES))


def _new_keys_init(m_ref, l_ref, acc_ref, h, q8, kn, vn, bias_cols, n_new):
    tok = lax.broadcasted_iota(jnp.int32, (8, 1), 0) % n_new
    cols = []
    for j in range(n_new):
        sj = jnp.sum(q8 * kn[j:j + 1, :], axis=-1, keepdims=True)
        if bias_cols is not None:
            sj = sj + bias_cols[j]
        cols.append(jnp.where(tok >= j, sj, -jnp.inf))
    m = cols[0]
    for sj in cols[1:]:
        m = jnp.maximum(m, sj)
    ps = [jnp.exp(sj - m) for sj in cols]
    l = ps[0]
    acc = ps[0] * vn[0:1, :]
    for j in range(1, n_new):
        l = l + ps[j]
        acc = acc + ps[j] * vn[j:j + 1, :]
    m_ref[h] = jnp.broadcast_to(m, (8, LANES))
    l_ref[h] = jnp.broadcast_to(l, (8, LANES))
    acc_ref[h] = acc


def _sdiff_body(pt_ref, lv_ref, g_ref, q_ref, kn_ref, vn_ref, *refs, n_pages, n_new, lam_init):
    k_refs = refs[:n_pages]
    v_refs = refs[n_pages:2 * n_pages]
    o_ref, m_ref, l_ref, acc_ref = refs[2 * n_pages:]
    grp = pl.program_id(1)
    lane = lax.broadcasted_iota(jnp.int32, (8, 128), 1)
    row = lax.broadcasted_iota(jnp.int32, (8, 128), 0)
    for h in range(DIFF_HEADS):
        lanes = slice(h * 128, (h + 1) * 128)
        qh = q_ref[:, lanes].astype(F32)
        q8 = jnp.concatenate([qh, qh], axis=0)
        q8 = jnp.where((lane < 64) == (row < n_new), q8, 0.0)

        @pl.when(grp == 0)
        def _(h=h, q8=q8, lanes=lanes):
            _new_keys_init(m_ref, l_ref, acc_ref, h, q8, kn_ref[:, lanes], vn_ref[:, lanes], None, n_new)

        kcat = jnp.concatenate([r[:, h, :] for r in k_refs], axis=0)
        vcat = jnp.concatenate([r[:, h, :] for r in v_refs], axis=0)
        _online_update(m_ref, l_ref, acc_ref, h, _dot_nt(q8, kcat), vcat, 128)

    @pl.when(grp == pl.num_programs(1) - 1)
    def _():
        lam = _diff_lambda(lv_ref, lam_init)
        for h in range(DIFF_HEADS):
            o = acc_ref[h] / l_ref[h]
            o = o[0:n_new] - lam * o[n_new:2 * n_new]
            o_ref[:, h * 128:(h + 1) * 128] = (_rms(o, g_ref[...]) * (1.0 - lam_init)).astype(BF16)


def _paged_specs(cache, l, n_pages):
    tail = cache.shape[2:]
    zeros = (0,) * len(tail)
    return [pl.BlockSpec((None, None) + tail,
                         lambda b, g, pt, i=i: (l, pt[b, g * n_pages + i]) + zeros)
            for i in range(n_pages)]


def _sample_diff(page_table, q, k_new, v_new, cache_k, cache_v, W, l, lam_init):
    nb, n_new, _ = q.shape
    n_groups = page_table.shape[1] // PAGES_PER_STEP
    per_b = pl.BlockSpec((None, n_new, WIDTH), lambda b, g, pt: (b, 0, 0))
    grid_spec = pltpu.PrefetchScalarGridSpec(
        num_scalar_prefetch=1,
        grid=(nb, n_groups),
        in_specs=[pl.BlockSpec((None, 4, DIFF_HEAD_DIM), lambda b, g, pt: (l, 0, 0)),
                  pl.BlockSpec((None, 1, 128), lambda b, g, pt: (l, 0, 0)),
                  per_b, per_b, per_b]
                 + _paged_specs(cache_k, l, PAGES_PER_STEP) + _paged_specs(cache_v, l, PAGES_PER_STEP),
        out_specs=per_b,
        scratch_shapes=[pltpu.VMEM((DIFF_HEADS, 8, LANES), F32), pltpu.VMEM((DIFF_HEADS, 8, LANES), F32),
                        pltpu.VMEM((DIFF_HEADS, 8, 128), F32)],
    )
    return pl.pallas_call(
        functools.partial(_sdiff_body, n_pages=PAGES_PER_STEP, n_new=n_new, lam_init=lam_init),
        grid_spec=grid_spec,
        out_shape=jax.ShapeDtypeStruct((nb, n_new, WIDTH), BF16),
        compiler_params=_params(("parallel", "arbitrary")),
        name="sample_diff_attn",
    )(page_table, W["diff_lambda"], W["g_subln"], q, k_new, v_new,
      *([cache_k] * PAGES_PER_STEP), *([cache_v] * PAGES_PER_STEP))


def _sfox_body(pt_ref, q_ref, kn_ref, vn_ref, lf_ref, cp_ref, *refs, n_pages, n_new):
    k_refs = refs[:n_pages]
    v_refs = refs[n_pages:2 * n_pages]
    o_ref, m_ref, l_ref, acc_ref = refs[2 * n_pages:]
    grp = pl.program_id(1)
    lf = lf_ref[...]
    f_rows = [lf[0:1]]
    for t in range(1, n_new):
        f_rows.append(f_rows[-1] + lf[t:t + 1])
    f_new = jnp.concatenate(f_rows + f_rows, axis=0)
    for h in range(FOX_HEADS):
        lanes = slice(h * FOX_HEAD_DIM, (h + 1) * FOX_HEAD_DIM)
        qh = q_ref[:, lanes].astype(F32)
        q8 = jnp.concatenate([qh, qh], axis=0)
        fq = f_new[:, h:h + 1]

        @pl.when(grp == 0)
        def _(h=h, q8=q8, lanes=lanes, fq=fq):
            bias_cols = [fq - f_rows[j][:, h:h + 1] for j in range(n_new)]
            _new_keys_init(m_ref, l_ref, acc_ref, h, q8, kn_ref[:, lanes], vn_ref[:, lanes], bias_cols, n_new)

        kcat = jnp.concatenate([r[:, h, :] for r in k_refs], axis=0)
        vcat = jnp.concatenate([r[:, h, :] for r in v_refs], axis=0)
        fk = jnp.concatenate([cp_ref[i, h:h + 1, :] for i in range(n_pages)], axis=1)
        _online_update(m_ref, l_ref, acc_ref, h, _dot_nt(q8, kcat) + (fq - fk), vcat, FOX_HEAD_DIM)

    @pl.when(grp == pl.num_programs(1) - 1)
    def _():
        for h in range(FOX_HEADS):
            o = acc_ref[h] / l_ref[h][:, 0:FOX_HEAD_DIM]
            o_ref[:, h * FOX_HEAD_DIM:(h + 1) * FOX_HEAD_DIM] = o[0:n_new].astype(BF16)


def _sample_fox(page_table, q, k_new, v_new, lf_new, c_past, cache_k, cache_v, l):
    nb, n_new, _ = q.shape
    n_groups = page_table.shape[1] // PAGES_PER_STEP
    page = c_past.shape[-1]
    per_b = pl.BlockSpec((None, n_new, WIDTH), lambda b, g, pt: (b, 0, 0))
    grid_spec = pltpu.PrefetchScalarGridSpec(
        num_scalar_prefetch=1,
        grid=(nb, n_groups),
        in_specs=[per_b, per_b, per_b,
                  pl.BlockSpec((None, n_new, FOX_HEADS), lambda b, g, pt: (b, 0, 0)),
                  pl.BlockSpec((None, PAGES_PER_STEP, FOX_HEADS, page), lambda b, g, pt: (b, g, 0, 0))]
                 + _paged_specs(cache_k, l, PAGES_PER_STEP) + _paged_specs(cache_v, l, PAGES_PER_STEP),
        out_specs=per_b,
        scratch_shapes=[pltpu.VMEM((FOX_HEADS, 8, LANES), F32), pltpu.VMEM((FOX_HEADS, 8, LANES), F32),
                        pltpu.VMEM((FOX_HEADS, 8, FOX_HEAD_DIM), F32)],
    )
    return pl.pallas_call(
        functools.partial(_sfox_body, n_pages=PAGES_PER_STEP, n_new=n_new),
        grid_spec=grid_spec,
        out_shape=jax.ShapeDtypeStruct((nb, n_new, WIDTH), BF16),
        compiler_params=_params(("parallel", "arbitrary")),
        name="sample_fox_attn",
    )(page_table, q, k_new, v_new, lf_new, c_past,
      *([cache_k] * PAGES_PER_STEP), *([cache_v] * PAGES_PER_STEP))


def _past_forget_body(pt_ref, lf_ref, o_ref, *, n_pages):
    b = pl.program_id(0)
    cums = [_lane_cumsum(lf_ref[pt_ref[b, p]]) for p in range(n_pages)]
    run = jnp.zeros((cums[0].shape[0], 1), F32)
    offsets = []
    for c in cums:
        offsets.append(run)
        run = run + c[:, -1:]
    for p in range(n_pages):
        o_ref[p] = (cums[p] + offsets[p]) - run


def _past_forget(page_table, logf_rows, l):
    nb, n_pages = page_table.shape
    _, n_pool, heads, page = logf_rows.shape
    grid_spec = pltpu.PrefetchScalarGridSpec(
        num_scalar_prefetch=1,
        grid=(nb,),
        in_specs=[pl.BlockSpec((None, n_pool, heads, page), lambda b, pt: (l, 0, 0, 0),
                               pipeline_mode=pl.Buffered(1))],
        out_specs=pl.BlockSpec((None, n_pages, heads, page), lambda b, pt: (b, 0, 0, 0)),
    )
    return pl.pallas_call(
        functools.partial(_past_forget_body, n_pages=n_pages),
        grid_spec=grid_spec,
        out_shape=jax.ShapeDtypeStruct((nb, n_pages, heads, page), F32),
        compiler_params=_params(("arbitrary",)),
        name="past_forget",
    )(page_table, logf_rows)


def _rope_tables(pos):
    half = DIFF_HEAD_DIM // 2
    inv = ROPE_THETA ** (-jnp.arange(half, dtype=F32) / half)
    ang = pos.astype(F32)[:, None] * inv[None, :]
    cos = jnp.cos(ang)
    sin = jnp.sin(ang)
    reps = WIDTH // DIFF_HEAD_DIM
    return (jnp.tile(jnp.concatenate([cos, cos], axis=-1), (1, reps)),
            jnp.tile(jnp.concatenate([-sin, sin], axis=-1), (1, reps)))


def _block_diag(w):
    depth, nblk, d, e = w.shape
    eye = jnp.eye(nblk, dtype=w.dtype)
    return jnp.einsum("lnde,nm->lndme", w, eye).reshape(depth, nblk * d, nblk * e)


def kernel(x_prompt, x_sample, cache_diff_k, cache_diff_v, cache_fox_k, cache_fox_v, cache_fox_logf, state_lru_h, state_conv, page_table, norm_ffn1, w_ffn1_in, w_ffn1_out, norm_mix, w_in, b_forget, diff_q_norm, diff_k_norm, diff_lambda_q1, diff_lambda_k1, diff_lambda_q2, diff_lambda_k2, diff_subln, fox_q_norm, fox_k_norm, conv_w, conv_b, lru_w_a, lru_b_a, lru_w_x, lru_b_x, lru_lambda, w_diff_out, w_fox_out, w_lru_out, w_o, norm_ffn2, w_ffn2_in, w_ffn2_out):
    depth = w_in.shape[0]
    bp, n_t, _ = x_prompt.shape
    nb, n_new, _ = x_sample.shape
    page = cache_diff_k.shape[2]
    past_len = page_table.shape[1] * page

    vec = lambda a: a[:, None, :]
    per_head = lambda a: jnp.tile(a, (1, WIDTH // a.shape[-1]))[:, None, :]
    c0 = 6 * WIDTH
    c1 = c0 + FOX_HEADS
    c2 = c1 + 2 * WIDTH
    W = {
        "norm_ffn1": vec(norm_ffn1), "norm_mix": vec(norm_mix), "norm_ffn2": vec(norm_ffn2),
        "w_ffn1_in": w_ffn1_in.astype(BF16), "w_ffn1_out": w_ffn1_out.astype(BF16),
        "w_ffn2_in": w_ffn2_in.astype(BF16), "w_ffn2_out": w_ffn2_out.astype(BF16),
        "w_qkv": jnp.concatenate(
            [w_in[:, :, :c0], w_in[:, :, c1:c2],
             jnp.pad(w_in[:, :, c0:c1], ((0, 0), (0, 0), (0, LANES - FOX_HEADS)))], axis=-1).astype(BF16),
        "w_gates": w_in[:, :, c2:].astype(BF16),
        "head_sum": jnp.kron(jnp.eye(WIDTH // DIFF_HEAD_DIM, dtype=F32),
                             jnp.ones((DIFF_HEAD_DIM, DIFF_HEAD_DIM), F32)).astype(BF16),
        "g_dq": per_head(diff_q_norm), "g_dk": per_head(diff_k_norm),
        "g_fq": per_head(fox_q_norm), "g_fk": per_head(fox_k_norm),
        "g_subln": vec(diff_subln),
        "b_forget": jnp.pad(b_forget, ((0, 0), (0, LANES - FOX_HEADS)))[:, None, :],
        "diff_lambda": jnp.stack([diff_lambda_q1, diff_lambda_k1, diff_lambda_q2, diff_lambda_k2], axis=1),
        "conv_w": conv_w, "conv_b": vec(conv_b),
        "lru_w_ax": jnp.concatenate([_block_diag(lru_w_a), _block_diag(lru_w_x)], axis=-1).astype(BF16),
        "lru_b_a": vec(lru_b_a), "lru_b_x": vec(lru_b_x), "lru_lambda": vec(lru_lambda),
        "w_diff_out": w_diff_out.astype(BF16), "w_fox_out": w_fox_out.astype(BF16),
        "w_lru_out": w_lru_out.astype(BF16), "w_o": w_o.astype(BF16),
    }
    logf_rows = jnp.swapaxes(cache_fox_logf, 2, 3)

    cos_p, sin_p = _rope_tables(jnp.arange(n_t))
    cos_s, sin_s = _rope_tables(jnp.tile(past_len + jnp.arange(n_new), nb))

    tm_p, t_attn, tt_lru = 512, 256, 512
    m_s = nb * n_new
    yp = x_prompt.reshape(bp * n_t, D_MODEL)
    ys = x_sample.reshape(m_s, D_MODEL)
    p_rows, s_rows = [], []
    for l in range(depth):
        lam_init = 0.8 - 0.6 * math.exp(-0.3 * l)

        yp = _ffn(yp, W["norm_ffn1"], W["w_ffn1_in"], W["w_ffn1_out"], l, tm_p)
        dq, dk, dkb, dv, dvb, fq, fk, fkb, fv, fvb, lf, lx, lg = _proj(
            yp, W, l, tm_p, cos_p, sin_p, n_t // tm_p)
        seq = lambda a: a.reshape(bp, n_t, a.shape[-1])
        d_out = _prompt_diff(seq(dq), seq(dkb), seq(dvb), W, l, lam_init, t_attn)
        f_row = _prompt_cum_forget(jnp.swapaxes(seq(lf), 1, 2))
        f_out = _prompt_fox(seq(fq), seq(fkb), seq(fvb), jnp.swapaxes(f_row, 1, 2), f_row, t_attn)
        l_out, h_last = _prompt_lru(seq(lx), seq(lg), W, l, tt_lru)
        flat = lambda a: a.reshape(bp * n_t, WIDTH)
        yp = _merge(yp, flat(d_out), flat(f_out), flat(l_out), W, l, tm_p)
        yp = _ffn(yp, W["norm_ffn2"], W["w_ffn2_in"], W["w_ffn2_out"], l, tm_p)
        p_rows.append((
            dk.reshape(bp, n_t, DIFF_HEADS, 2 * DIFF_HEAD_DIM), dv.reshape(bp, n_t, DIFF_HEADS, 2 * DIFF_HEAD_DIM),
            fk.reshape(bp, n_t, FOX_HEADS, FOX_HEAD_DIM), fv.reshape(bp, n_t, FOX_HEADS, FOX_HEAD_DIM),
            seq(lf), h_last.reshape(bp, WIDTH), seq(lx)[:, n_t - (CONV_WIDTH - 1):, :]))

        ys = _ffn(ys, W["norm_ffn1"], W["w_ffn1_in"], W["w_ffn1_out"], l, m_s)
        dq, dk, dkb, dv, dvb, fq, fk, fkb, fv, fvb, lf, lx, lg = _proj(ys, W, l, m_s, cos_s, sin_s, 1)
        tok = lambda a: a.reshape(nb, n_new, a.shape[-1])
        d_out = _sample_diff(page_table, tok(dq), tok(dk), tok(dv), cache_diff_k, cache_diff_v, W, l, lam_init)
        c_past = _past_forget(page_table, logf_rows, l)
        f_out = _sample_fox(page_table, tok(fq), tok(fk), tok(fv), tok(lf), c_past, cache_fox_k, cache_fox_v, l)
        lo_t, h_last = _sample_lru(jnp.swapaxes(tok(lx), 0, 1), jnp.swapaxes(tok(lg), 0, 1),
                                   jnp.swapaxes(state_conv[l], 0, 1), state_lru_h[l], W, l)
        l_out = jnp.swapaxes(lo_t, 0, 1)
        flat = lambda a: a.reshape(m_s, WIDTH)
        ys = _merge(ys, flat(d_out), flat(f_out), flat(l_out), W, l, m_s)
        ys = _ffn(ys, W["norm_ffn2"], W["w_ffn2_in"], W["w_ffn2_out"], l, m_s)
        conv_last = jnp.concatenate([state_conv[l], tok(lx)], axis=1)[:, n_new:, :]
        s_rows.append((
            dk.reshape(nb, n_new, DIFF_HEADS, 2 * DIFF_HEAD_DIM), dv.reshape(nb, n_new, DIFF_HEADS, 2 * DIFF_HEAD_DIM),
            fk.reshape(nb, n_new, FOX_HEADS, FOX_HEAD_DIM), fv.reshape(nb, n_new, FOX_HEADS, FOX_HEAD_DIM),
            tok(lf), h_last, conv_last))

    stack = lambda rows, i: jnp.stack([r[i] for r in rows])
    return ((yp.reshape(bp, n_t, D_MODEL), ys.reshape(nb, n_new, D_MODEL))
            + tuple(stack(p_rows, i) for i in range(7)) + tuple(stack(s_rows, i) for i in range(7)))
```

```python
import functools
import math

import jax
import jax.numpy as jnp
from jax import lax
from jax.experimental import pallas as pl
from jax.experimental.pallas import tpu as pltpu

F32 = jnp.float32
BF16 = jnp.bfloat16

D_MODEL = 1024
DIFF_HEADS = 4
DIFF_HEAD_DIM = 64
FOX_HEADS = 8
FOX_HEAD_DIM = 64
WIDTH = 512
HEAD_GROUPS = WIDTH // 128
LRU_C = 8.0
CONV_WIDTH = 4
D_FF = 2816
N_BRANCH = 3
ROPE_THETA = 10000.0
EPS = 1e-6
QKV_COLS = 8 * WIDTH + 128

V7X_VMEM_LIMIT = 56 * 1024 * 1024
LANES = 128


def _params(semantics):
    return pltpu.CompilerParams(dimension_semantics=semantics, vmem_limit_bytes=V7X_VMEM_LIMIT)


def _resident(shape, index_map):
    return pl.BlockSpec(shape, index_map, pipeline_mode=pl.Buffered(1))


def _rms(x, g):
    return x * lax.rsqrt(jnp.mean(x * x, axis=-1, keepdims=True) + EPS) * g


def _dot(a, b):
    return jnp.dot(a, b, preferred_element_type=F32)


def _dot_nt(a, b):
    return lax.dot_general(a, b, (((1,), (1,)), ((), ())), preferred_element_type=F32)


def _softplus(x):
    return jnp.maximum(x, 0.0) + jnp.log1p(jnp.exp(-jnp.abs(x)))


def _gelu_tanh(x):
    return 0.5 * x * (1.0 + jnp.tanh(math.sqrt(2.0 / math.pi) * (x + 0.044715 * (x * x * x))))


def _ffn_body(x_ref, g_ref, wg_ref, wu_ref, wo_ref, o_ref):
    x = x_ref[...]
    xb = _rms(x, g_ref[...]).astype(BF16)
    g = _dot(xb, wg_ref[...])
    u = _dot(xb, wu_ref[...])
    act = (g * jax.nn.sigmoid(g) * u).astype(BF16)
    o_ref[...] = x + 0.5 * _dot(act, wo_ref[...])


def _ffn(x, gain, w_in, w_out, l, tm):
    m = x.shape[0]
    return pl.pallas_call(
        _ffn_body,
        grid=(m // tm,),
        in_specs=[
            pl.BlockSpec((tm, D_MODEL), lambda i: (i, 0)),
            _resident((None, 1, D_MODEL), lambda i: (l, 0, 0)),
            _resident((None, D_MODEL, D_FF), lambda i: (l, 0, 0)),
            _resident((None, D_MODEL, D_FF), lambda i: (l, 0, 1)),
            _resident((None, D_FF, D_MODEL), lambda i: (l, 0, 0)),
        ],
        out_specs=pl.BlockSpec((tm, D_MODEL), lambda i: (i, 0)),
        out_shape=jax.ShapeDtypeStruct((m, D_MODEL), F32),
        compiler_params=_params(("parallel",)),
        name="ffn",
    )(x, gain, w_in, w_in, w_out)


def _proj_body(x_ref, g_ref, w_ref, bd_ref, cos_ref, sin_ref, gdq_ref, gdk_ref, gfq_ref, gfk_ref, bf_ref,
               dq_o, dk_o, dkb_o, dv_o, dvb_o, fq_o, fk_o, fkb_o, fv_o, fvb_o, lf_o, lx_o, lg_o):
    tm = x_ref.shape[0]
    xb = _rms(x_ref[...], g_ref[...]).astype(BF16)
    y = _dot(xb, w_ref[...])
    bd = bd_ref[...]

    def head_rms(t, g):
        ss = _dot((t * t).astype(BF16), bd)
        return t * lax.rsqrt(ss * (1.0 / DIFF_HEAD_DIM) + EPS) * g

    lane = lax.broadcasted_iota(jnp.int32, (tm, WIDTH), 1)
    first_half = (lane & (DIFF_HEAD_DIM // 2)) == 0
    cos = cos_ref[...]
    sin = sin_ref[...]

    def rope(t):
        partner = jnp.where(first_half, pltpu.roll(t, WIDTH - DIFF_HEAD_DIM // 2, 1),
                            pltpu.roll(t, DIFF_HEAD_DIM // 2, 1))
        return t * cos + partner * sin

    scale = DIFF_HEAD_DIM ** -0.5
    dq = rope(head_rms(y[:, 0 * WIDTH:1 * WIDTH], gdq_ref[...]))
    dk = rope(head_rms(y[:, 1 * WIDTH:2 * WIDTH], gdk_ref[...]))
    dv = y[:, 2 * WIDTH:3 * WIDTH]
    fq = head_rms(y[:, 3 * WIDTH:4 * WIDTH], gfq_ref[...])
    fk = head_rms(y[:, 4 * WIDTH:5 * WIDTH], gfk_ref[...])
    fv = y[:, 5 * WIDTH:6 * WIDTH]
    z = y[:, 8 * WIDTH:8 * WIDTH + LANES] + bf_ref[...]
    lf = jnp.minimum(z, 0.0) - jnp.log1p(jnp.exp(-jnp.abs(z)))

    dq_o[...] = (dq * scale).astype(BF16)
    dk_o[...] = dk
    dkb_o[...] = dk.astype(BF16)
    dv_o[...] = dv
    dvb_o[...] = dv.astype(BF16)
    fq_o[...] = (fq * scale).astype(BF16)
    fk_o[...] = fk
    fkb_o[...] = fk.astype(BF16)
    fv_o[...] = fv
    fvb_o[...] = fv.astype(BF16)
    lf_o[...] = lf[:, :FOX_HEADS]
    lx_o[...] = y[:, 6 * WIDTH:7 * WIDTH]
    lg_o[...] = y[:, 7 * WIDTH:8 * WIDTH]


def _proj(x, W, l, tm, cos, sin, n_pos_blocks):
    m = x.shape[0]
    row = lambda width: pl.BlockSpec((tm, width), lambda i: (i, 0))
    vec = lambda width: _resident((None, 1, width), lambda i: (l, 0, 0))
    pos = pl.BlockSpec((tm, WIDTH), lambda i: (i % n_pos_blocks, 0))
    f32w = jax.ShapeDtypeStruct((m, WIDTH), F32)
    b16w = jax.ShapeDtypeStruct((m, WIDTH), BF16)
    return pl.pallas_call(
        _proj_body,
        grid=(m // tm,),
        in_specs=[
            row(D_MODEL), vec(D_MODEL),
            _resident((None, D_MODEL, QKV_COLS), lambda i: (l, 0, 0)),
            _resident((WIDTH, WIDTH), lambda i: (0, 0)),
            pos, pos, vec(WIDTH), vec(WIDTH), vec(WIDTH), vec(WIDTH), vec(LANES),
        ],
        out_specs=[row(WIDTH)] * 10 + [row(FOX_HEADS), row(WIDTH), row(WIDTH)],
        out_shape=[b16w, f32w, b16w, f32w, b16w, b16w, f32w, b16w, f32w, b16w,
                   jax.ShapeDtypeStruct((m, FOX_HEADS), F32), f32w, f32w],
        compiler_params=_params(("parallel",)),
        name="proj",
    )(x, W["norm_mix"], W["w_qkv"], W["head_sum"], cos, sin,
      W["g_dq"], W["g_dk"], W["g_fq"], W["g_fk"], W["b_forget"])


def _merge_body(x_ref, g_ref, wg_ref, d_ref, f_ref, r_ref, wd_ref, wf_ref, wl_ref, wo_ref, o_ref):
    x = x_ref[...]
    xb = _rms(x, g_ref[...]).astype(BF16)
    gates = jax.nn.sigmoid(_dot(xb, wg_ref[...]))
    merged = (gates[:, 0:D_MODEL] * _dot(d_ref[...], wd_ref[...])
              + gates[:, D_MODEL:2 * D_MODEL] * _dot(f_ref[...], wf_ref[...])
              + gates[:, 2 * D_MODEL:] * _dot(r_ref[...], wl_ref[...]))
    o_ref[...] = x + _dot(merged.astype(BF16), wo_ref[...])


def _merge(x, d_out, f_out, l_out, W, l, tm):
    m = x.shape[0]
    row = lambda width: pl.BlockSpec((tm, width), lambda i: (i, 0))
    mat = lambda r, c: _resident((None, r, c), lambda i: (l, 0, 0))
    return pl.pallas_call(
        _merge_body,
        grid=(m // tm,),
        in_specs=[row(D_MODEL), mat(1, D_MODEL), mat(D_MODEL, N_BRANCH * D_MODEL),
                  row(WIDTH), row(WIDTH), row(WIDTH),
                  mat(WIDTH, D_MODEL), mat(WIDTH, D_MODEL), mat(WIDTH, D_MODEL), mat(D_MODEL, D_MODEL)],
        out_specs=row(D_MODEL),
        out_shape=jax.ShapeDtypeStruct((m, D_MODEL), F32),
        compiler_params=_params(("parallel",)),
        name="merge",
    )(x, W["norm_mix"], W["w_gates"], d_out, f_out, l_out,
      W["w_diff_out"], W["w_fox_out"], W["w_lru_out"], W["w_o"])


def _stack_halves(qg):
    lane = lax.broadcasted_iota(jnp.int32, qg.shape, 1)
    zero = jnp.zeros_like(qg)
    return jnp.concatenate([jnp.where(lane < 64, qg, zero), jnp.where(lane >= 64, qg, zero)], axis=0)


def _flash_groups(qqs, k_ref, vt_ref, n_full, t, bias_fns):
    cols = 2 * t
    key = lax.broadcasted_iota(jnp.int32, (t, cols), 0)
    qry = lax.broadcasted_iota(jnp.int32, (t, cols), 1)
    causal = key <= jnp.where(qry >= t, qry - t, qry)

    def step(j, carry, masked):
        start = pl.multiple_of(j * t, t)
        out = []
        for grp, (m, l, acc) in enumerate(carry):
            lanes = slice(grp * 128, (grp + 1) * 128)
            s = _dot_nt(k_ref[pl.ds(start, t), lanes], qqs[grp])
            if bias_fns is not None:
                s = bias_fns[grp](s, start)
            if masked:
                s = jnp.where(causal, s, -jnp.inf)
            m_new = jnp.maximum(m, jnp.max(s, axis=0, keepdims=True))
            alpha = jnp.exp(m - m_new)
            p = jnp.exp(s - m_new)
            l = alpha * l + jnp.sum(p, axis=0, keepdims=True)
            acc = alpha * acc + _dot(vt_ref[j, lanes, :], p.astype(BF16))
            out.append((m_new, l, acc))
        return tuple(out)

    init = tuple((jnp.full((1, cols), -jnp.inf, F32), jnp.zeros((1, cols), F32), jnp.zeros((128, cols), F32))
                 for _ in qqs)
    carry = lax.fori_loop(0, n_full, lambda j, c: step(j, c, False), init)
    return [(l, acc) for _, l, acc in step(n_full, carry, True)]


def _diff_lambda(lv_ref, lam_init):
    lv = lv_ref[...]
    s1 = jnp.sum(lv[0:1] * lv[1:2], axis=-1, keepdims=True)
    s2 = jnp.sum(lv[2:3] * lv[3:4], axis=-1, keepdims=True)
    return jnp.exp(s1) - jnp.exp(s2) + lam_init


def _pdiff_body(lv_ref, g_ref, q_ref, k_ref, vt_ref, o_ref, *, t, lam_init):
    i = pl.program_id(1)
    lam = _diff_lambda(lv_ref, lam_init)
    qqs = [_stack_halves(q_ref[:, h * 128:(h + 1) * 128]) for h in range(DIFF_HEADS)]
    for h, (l, acc) in enumerate(_flash_groups(qqs, k_ref, vt_ref, i, t, None)):
        o = acc[:, :t] / l[:, :t] - lam * (acc[:, t:] / l[:, t:])
        o = o * lax.rsqrt(jnp.mean(o * o, axis=0, keepdims=True) + EPS) * g_ref[...]
        o_ref[:, h * 128:(h + 1) * 128] = (o * (1.0 - lam_init)).T.astype(BF16)


def _kv_tiles_t(v, t):
    b, n_t, w = v.shape
    return jnp.swapaxes(v.reshape(b, n_t // t, t, w), 2, 3)


def _prompt_diff(q, k, v, W, l, lam_init, t):
    b, n_t, _ = q.shape
    seq = pl.BlockSpec((None, n_t, WIDTH), lambda bi, i: (bi, 0, 0))
    seq_t = pl.BlockSpec((None, n_t // t, WIDTH, t), lambda bi, i: (bi, 0, 0, 0))
    tile = pl.BlockSpec((None, t, WIDTH), lambda bi, i: (bi, i, 0))
    return pl.pallas_call(
        functools.partial(_pdiff_body, t=t, lam_init=lam_init),
        grid=(b, n_t // t),
        in_specs=[pl.BlockSpec((None, 4, DIFF_HEAD_DIM), lambda bi, i: (l, 0, 0)),
                  pl.BlockSpec((None, 128, 1), lambda bi, i: (l, 0, 0)),
                  tile, seq, seq_t],
        out_specs=tile,
        out_shape=jax.ShapeDtypeStruct((b, n_t, WIDTH), BF16),
        compiler_params=_params(("parallel", "arbitrary")),
        name="prompt_diff_attn",
    )(W["diff_lambda"], W["g_subln_col"], q, k, _kv_tiles_t(v, t))


def _pfox_body(q_ref, k_ref, vt_ref, fq_ref, fk_ref, o_ref, *, t):
    i = pl.program_id(1)
    dim = lax.broadcasted_iota(jnp.int32, (128, t), 0)
    qqs = [_stack_halves(q_ref[:, grp * 128:(grp + 1) * 128]) for grp in range(HEAD_GROUPS)]

    def make_bias(grp):
        fq0 = fq_ref[2 * grp, pl.ds(i, 1), :]
        fq1 = fq_ref[2 * grp + 1, pl.ds(i, 1), :]

        def bias(s, start):
            fk0 = fk_ref[pl.ds(start, t), 2 * grp:2 * grp + 1]
            fk1 = fk_ref[pl.ds(start, t), 2 * grp + 1:2 * grp + 2]
            return jnp.concatenate([s[:, :t] + (fq0 - fk0), s[:, t:] + (fq1 - fk1)], axis=1)

        return bias

    biases = [make_bias(grp) for grp in range(HEAD_GROUPS)]
    for grp, (l, acc) in enumerate(_flash_groups(qqs, k_ref, vt_ref, i, t, biases)):
        o = jnp.where(dim < 64, acc[:, :t] / l[:, :t], acc[:, t:] / l[:, t:])
        o_ref[:, grp * 128:(grp + 1) * 128] = o.T.astype(BF16)


def _prompt_fox(q, k, v, f_col, f_row, t):
    b, n_t, _ = q.shape
    seq = pl.BlockSpec((None, n_t, WIDTH), lambda bi, i: (bi, 0, 0))
    seq_t = pl.BlockSpec((None, n_t // t, WIDTH, t), lambda bi, i: (bi, 0, 0, 0))
    tile = pl.BlockSpec((None, t, WIDTH), lambda bi, i: (bi, i, 0))
    return pl.pallas_call(
        functools.partial(_pfox_body, t=t),
        grid=(b, n_t // t),
        in_specs=[tile, seq, seq_t,
                  pl.BlockSpec((None, FOX_HEADS, n_t // t, t), lambda bi, i: (bi, 0, 0, 0)),
                  pl.BlockSpec((None, n_t, FOX_HEADS), lambda bi, i: (bi, 0, 0))],
        out_specs=tile,
        out_shape=jax.ShapeDtypeStruct((b, n_t, WIDTH), BF16),
        compiler_params=_params(("parallel", "arbitrary")),
        name="prompt_fox_attn",
    )(q, k, _kv_tiles_t(v, t), f_row.reshape(b, FOX_HEADS, n_t // t, t), f_col)


def _lane_cumsum(x):
    n = x.shape[-1]
    lane = lax.broadcasted_iota(jnp.int32, x.shape, x.ndim - 1)
    s = 1
    while s < n:
        x = x + jnp.where(lane >= s, pltpu.roll(x, s, x.ndim - 1), 0.0)
        s *= 2
    return x


def _cumsum_body(x_ref, o_ref):
    o_ref[...] = _lane_cumsum(x_ref[...])


def _prompt_cum_forget(lf_rows):
    b, h, n_t = lf_rows.shape
    spec = pl.BlockSpec((None, h, n_t), lambda bi: (bi, 0, 0))
    return pl.pallas_call(
        _cumsum_body, grid=(b,), in_specs=[spec], out_specs=spec,
        out_shape=jax.ShapeDtypeStruct((b, h, n_t), F32),
        compiler_params=_params(("parallel",)), name="cum_forget",
    )(lf_rows)


def _lru_gates(xc, wax_ref, ba_ref, bx_ref, lam_ref):
    ga = _dot(xc.astype(BF16), wax_ref[...])
    r = jax.nn.sigmoid(ga[:, :WIDTH] + ba_ref[...])
    i = jax.nn.sigmoid(ga[:, WIDTH:] + bx_ref[...])
    log_a = -LRU_C * r * _softplus(-lam_ref[...])
    a = jnp.exp(log_a)
    u = jnp.sqrt(-jnp.tanh(log_a) * (a * a + 1.0)) * (i * xc)
    return a, u


def _plru_body(lx_ref, lg_ref, cw_ref, cb_ref, wax_ref, ba_ref, bx_ref, lam_ref, lo_ref, hl_ref,
               xs_ref, h_ref, *, tt):
    @pl.when(pl.program_id(1) == 0)
    def _():
        xs_ref[0:8, :] = jnp.zeros((8, WIDTH), F32)
        h_ref[...] = jnp.zeros((8, WIDTH), F32)

    x = lx_ref[...]
    xs_ref[8:8 + tt, :] = x
    cw = cw_ref[...]
    xc = cb_ref[...] + xs_ref[5:5 + tt, :] * cw[0:1]
    xc = xc + xs_ref[6:6 + tt, :] * cw[1:2]
    xc = xc + xs_ref[7:7 + tt, :] * cw[2:3]
    xc = xc + x * cw[3:4]
    xs_ref[0:8, :] = x[tt - 8:tt]

    a, u = _lru_gates(xc, wax_ref, ba_ref, bx_ref, lam_ref)
    row = lax.broadcasted_iota(jnp.int32, (tt, WIDTH), 0)
    s = 1
    while s < tt:
        keep = row >= s
        u = a * jnp.where(keep, pltpu.roll(u, s, 0), 0.0) + u
        a = a * jnp.where(keep, pltpu.roll(a, s, 0), 1.0)
        s *= 2
    hs = a * h_ref[0:1, :] + u
    h_ref[0:1, :] = hs[tt - 1:tt]
    hl_ref[...] = hs[tt - 1:tt]
    lo_ref[...] = (hs * _gelu_tanh(lg_ref[...])).astype(BF16)


def _prompt_lru(lx, lg, W, l, tt):
    b, n_t, _ = lx.shape
    tile = pl.BlockSpec((None, tt, WIDTH), lambda bi, i: (bi, i, 0))
    mat = lambda r, c: _resident((None, r, c), lambda bi, i: (l, 0, 0))
    return pl.pallas_call(
        functools.partial(_plru_body, tt=tt),
        grid=(b, n_t // tt),
        in_specs=[tile, tile, mat(CONV_WIDTH, WIDTH), mat(1, WIDTH), mat(WIDTH, 2 * WIDTH),
                  mat(1, WIDTH), mat(1, WIDTH), mat(1, WIDTH)],
        out_specs=[tile, pl.BlockSpec((None, 1, WIDTH), lambda bi, i: (bi, 0, 0))],
        out_shape=[jax.ShapeDtypeStruct((b, n_t, WIDTH), BF16), jax.ShapeDtypeStruct((b, 1, WIDTH), F32)],
        scratch_shapes=[pltpu.VMEM((tt + 8, WIDTH), F32), pltpu.VMEM((8, WIDTH), F32)],
        compiler_params=_params(("parallel", "arbitrary")),
        name="prompt_lru",
    )(lx, lg, W["conv_w"], W["conv_b"], W["lru_w_ax"], W["lru_b_a"], W["lru_b_x"], W["lru_lambda"])


def _slru_body(lx_ref, lg_ref, c0_ref, h0_ref, cw_ref, cb_ref, wax_ref, ba_ref, bx_ref, lam_ref,
               lo_ref, hl_ref, *, n_t, nb):
    cw = cw_ref[...]
    xs = [c0_ref[j] for j in range(CONV_WIDTH - 1)] + [lx_ref[t] for t in range(n_t)]
    xcs = []
    for t in range(n_t):
        xc = cb_ref[...] + xs[t] * cw[0:1]
        for j in range(1, CONV_WIDTH):
            xc = xc + xs[t + j] * cw[j:j + 1]
        xcs.append(xc)
    a, u = _lru_gates(jnp.concatenate(xcs, axis=0), wax_ref, ba_ref, bx_ref, lam_ref)
    h = h0_ref[...]
    for t in range(n_t):
        h = a[t * nb:(t + 1) * nb] * h + u[t * nb:(t + 1) * nb]
        lo_ref[t] = (h * _gelu_tanh(lg_ref[t])).astype(BF16)
    hl_ref[...] = h


def _sample_lru(lx_t, lg_t, conv0_t, h0, W, l):
    n_t, nb, _ = lx_t.shape
    full = lambda *shape: pl.BlockSpec(shape, lambda i: (0,) * len(shape))
    mat = lambda r, c: pl.BlockSpec((None, r, c), lambda i: (l, 0, 0))
    return pl.pallas_call(
        functools.partial(_slru_body, n_t=n_t, nb=nb),
        grid=(1,),
        in_specs=[full(n_t, nb, WIDTH), full(n_t, nb, WIDTH), full(CONV_WIDTH - 1, nb, WIDTH), full(nb, WIDTH),
                  mat(CONV_WIDTH, WIDTH), mat(1, WIDTH), mat(WIDTH, 2 * WIDTH),
                  mat(1, WIDTH), mat(1, WIDTH), mat(1, WIDTH)],
        out_specs=[full(n_t, nb, WIDTH), full(nb, WIDTH)],
        out_shape=[jax.ShapeDtypeStruct((n_t, nb, WIDTH), BF16), jax.ShapeDtypeStruct((nb, WIDTH), F32)],
        compiler_params=_params(("arbitrary",)),
        name="sample_lru",
    )(lx_t, lg_t, conv0_t, h0, W["conv_w"], W["conv_b"], W["lru_w_ax"], W["lru_b_a"], W["lru_b_x"],
      W["lru_lambda"])


PAGES_PER_STEP = 16


def _online_step(state, s, pv):
    m_old, l_old, acc_old = state
    m_new = jnp.maximum(m_old, jnp.max(s, axis=-1, keepdims=True))
    alpha = jnp.exp(m_old - m_new)
    p = jnp.exp(s - m_new)
    return m_new, alpha * l_old + jnp.sum(p, axis=-1, keepdims=True), alpha * acc_old + pv(p)


def _load_state(m_ref, l_ref, acc_ref, h):
    return m_ref[h][:, 0:1], l_ref[h][:, 0:1], acc_ref[h]


def _store_state(m_ref, l_ref, acc_ref, h, state):
    m, l, acc = state
    m_ref[h] = jnp.broadcast_to(m, (8, LANES))
    l_ref[h] = jnp.broadcast_to(l, (8, LANES))
    acc_ref[h] = acc


def _new_keys_state(q8, kn, vn, bias_cols, n_new):
    tok = lax.broadcasted_iota(jnp.int32, (8, 1), 0) % n_new
    cols = []
    for j in range(n_new):
        sj = jnp.sum(q8 * kn[j:j + 1, :], axis=-1, keepdims=True)
        if bias_cols is not None:
            sj = sj + bias_cols[j]
        cols.append(jnp.where(tok >= j, sj, -jnp.inf))
    m = cols[0]
    for sj in cols[1:]:
        m = jnp.maximum(m, sj)
    ps = [jnp.exp(sj - m) for sj in cols]
    l = ps[0]
    acc = ps[0] * vn[0:1, :]
    for j in range(1, n_new):
        l = l + ps[j]
        acc = acc + ps[j] * vn[j:j + 1, :]
    return m, l, acc


def _sdiff_body(pt_ref, lv_ref, g_ref, q_ref, kn_ref, vn_ref, *refs, n_pages, n_new, lam_init):
    k_refs = refs[:n_pages]
    v_refs = refs[n_pages:2 * n_pages]
    o_ref, m_ref, l_ref, acc_ref = refs[2 * n_pages:]
    state_refs = (m_ref, l_ref, acc_ref)
    grp = pl.program_id(1)
    lane = lax.broadcasted_iota(jnp.int32, (8, 128), 1)
    row = lax.broadcasted_iota(jnp.int32, (8, 128), 0)
    q8s = []
    for h in range(DIFF_HEADS):
        qh = q_ref[:, h * 128:(h + 1) * 128].astype(F32)
        q8 = jnp.concatenate([qh, qh], axis=0)
        q8s.append(jnp.where((lane < 64) == (row < n_new), q8, 0.0))

    @pl.when(grp == 0)
    def _():
        for h in range(DIFF_HEADS):
            lanes = slice(h * 128, (h + 1) * 128)
            _store_state(*state_refs, h,
                         _new_keys_state(q8s[h], kn_ref[:, lanes], vn_ref[:, lanes], None, n_new))

    page = k_refs[0].shape[0] // DIFF_HEADS
    old = [_load_state(*state_refs, h) for h in range(DIFF_HEADS)]
    new = []
    for h in range(DIFF_HEADS):
        head_rows = pl.ds(h, page, stride=DIFF_HEADS)
        kcat = jnp.concatenate([r[head_rows, :] for r in k_refs], axis=0)
        vcat = jnp.concatenate([r[head_rows, :] for r in v_refs], axis=0)
        new.append(_online_step(old[h], _dot_nt(q8s[h], kcat), lambda p, vcat=vcat: _dot(p, vcat)))
    for h in range(DIFF_HEADS):
        _store_state(*state_refs, h, new[h])

    @pl.when(grp == pl.num_programs(1) - 1)
    def _():
        lam = _diff_lambda(lv_ref, lam_init)
        for h in range(DIFF_HEADS):
            o = acc_ref[h] / l_ref[h]
            o = o[0:n_new] - lam * o[n_new:2 * n_new]
            o_ref[:, h * 128:(h + 1) * 128] = (_rms(o, g_ref[...]) * (1.0 - lam_init)).astype(BF16)


def _paged_specs(cache, l, n_pages):
    tail = cache.shape[2:]
    zeros = (0,) * len(tail)
    return [pl.BlockSpec((None, None) + tail,
                         lambda b, g, pt, i=i: (l, pt[b, g * n_pages + i]) + zeros)
            for i in range(n_pages)]


def _sample_diff(page_table, q, k_new, v_new, cache_k, cache_v, W, l, lam_init):
    nb, n_new, _ = q.shape
    n_groups = page_table.shape[1] // PAGES_PER_STEP
    per_b = pl.BlockSpec((None, n_new, WIDTH), lambda b, g, pt: (b, 0, 0))
    grid_spec = pltpu.PrefetchScalarGridSpec(
        num_scalar_prefetch=1,
        grid=(nb, n_groups),
        in_specs=[pl.BlockSpec((None, 4, DIFF_HEAD_DIM), lambda b, g, pt: (l, 0, 0)),
                  pl.BlockSpec((None, 1, 128), lambda b, g, pt: (l, 0, 0)),
                  per_b, per_b, per_b]
                 + _paged_specs(cache_k, l, PAGES_PER_STEP) + _paged_specs(cache_v, l, PAGES_PER_STEP),
        out_specs=per_b,
        scratch_shapes=[pltpu.VMEM((DIFF_HEADS, 8, LANES), F32), pltpu.VMEM((DIFF_HEADS, 8, LANES), F32),
                        pltpu.VMEM((DIFF_HEADS, 8, 128), F32)],
    )
    return pl.pallas_call(
        functools.partial(_sdiff_body, n_pages=PAGES_PER_STEP, n_new=n_new, lam_init=lam_init),
        grid_spec=grid_spec,
        out_shape=jax.ShapeDtypeStruct((nb, n_new, WIDTH), BF16),
        compiler_params=_params(("parallel", "arbitrary")),
        name="sample_diff_attn",
    )(page_table, W["diff_lambda"], W["g_subln"], q, k_new, v_new,
      *([cache_k] * PAGES_PER_STEP), *([cache_v] * PAGES_PER_STEP))


def _sfox_body(pt_ref, q_ref, kn_ref, vn_ref, lf_ref, cp_ref, *refs, n_pages, n_new):
    k_refs = refs[:n_pages]
    v_refs = refs[n_pages:2 * n_pages]
    o_ref, m_ref, l_ref, acc_ref = refs[2 * n_pages:]
    state_refs = (m_ref, l_ref, acc_ref)
    grp = pl.program_id(1)
    lf = lf_ref[...]
    f_rows = [lf[0:1]]
    for t in range(1, n_new):
        f_rows.append(f_rows[-1] + lf[t:t + 1])
    f_new = jnp.concatenate(f_rows + f_rows, axis=0)
    q8s = []
    for h in range(FOX_HEADS):
        qh = q_ref[:, h * FOX_HEAD_DIM:(h + 1) * FOX_HEAD_DIM].astype(F32)
        q8s.append(jnp.concatenate([qh, qh], axis=0))

    @pl.when(grp == 0)
    def _():
        for h in range(FOX_HEADS):
            lanes = slice(h * FOX_HEAD_DIM, (h + 1) * FOX_HEAD_DIM)
            bias_cols = [f_new[:, h:h + 1] - f_rows[j][:, h:h + 1] for j in range(n_new)]
            _store_state(*state_refs, h,
                         _new_keys_state(q8s[h], kn_ref[:, lanes], vn_ref[:, lanes], bias_cols, n_new))

    old = [_load_state(*state_refs, h) for h in range(FOX_HEADS)]
    new = []
    for h in range(FOX_HEADS):
        kt = jnp.concatenate([r[h] for r in k_refs], axis=1)
        vt = jnp.concatenate([r[h] for r in v_refs], axis=1)
        fk = jnp.concatenate([cp_ref[i, h:h + 1, :] for i in range(n_pages)], axis=1)
        s = _dot(q8s[h], kt) + (f_new[:, h:h + 1] - fk)
        new.append(_online_step(old[h], s, lambda p, vt=vt: _dot_nt(p, vt)))
    for h in range(FOX_HEADS):
        _store_state(*state_refs, h, new[h])

    @pl.when(grp == pl.num_programs(1) - 1)
    def _():
        for h in range(FOX_HEADS):
            o = acc_ref[h] / l_ref[h][:, 0:FOX_HEAD_DIM]
            o_ref[:, h * FOX_HEAD_DIM:(h + 1) * FOX_HEAD_DIM] = o[0:n_new].astype(BF16)


def _sample_fox(page_table, q, k_new, v_new, lf_new, c_past, cache_k, cache_v, l):
    nb, n_new, _ = q.shape
    n_groups = page_table.shape[1] // PAGES_PER_STEP
    page = c_past.shape[-1]
    per_b = pl.BlockSpec((None, n_new, WIDTH), lambda b, g, pt: (b, 0, 0))
    grid_spec = pltpu.PrefetchScalarGridSpec(
        num_scalar_prefetch=1,
        grid=(nb, n_groups),
        in_specs=[per_b, per_b, per_b,
                  pl.BlockSpec((None, n_new, FOX_HEADS), lambda b, g, pt: (b, 0, 0)),
                  pl.BlockSpec((None, PAGES_PER_STEP, FOX_HEADS, page), lambda b, g, pt: (b, g, 0, 0))]
                 + _paged_specs(cache_k, l, PAGES_PER_STEP) + _paged_specs(cache_v, l, PAGES_PER_STEP),
        out_specs=per_b,
        scratch_shapes=[pltpu.VMEM((FOX_HEADS, 8, LANES), F32), pltpu.VMEM((FOX_HEADS, 8, LANES), F32),
                        pltpu.VMEM((FOX_HEADS, 8, FOX_HEAD_DIM), F32)],
    )
    return pl.pallas_call(
        functools.partial(_sfox_body, n_pages=PAGES_PER_STEP, n_new=n_new),
        grid_spec=grid_spec,
        out_shape=jax.ShapeDtypeStruct((nb, n_new, WIDTH), BF16),
        compiler_params=_params(("parallel", "arbitrary")),
        name="sample_fox_attn",
    )(page_table, q, k_new, v_new, lf_new, c_past,
      *([cache_k] * PAGES_PER_STEP), *([cache_v] * PAGES_PER_STEP))


def _past_forget_body(pt_ref, lf_ref, o_ref, *, n_pages):
    b = pl.program_id(0)
    cums = [_lane_cumsum(lf_ref[pt_ref[b, p]]) for p in range(n_pages)]
    run = jnp.zeros((cums[0].shape[0], 1), F32)
    offsets = []
    for c in cums:
        offsets.append(run)
        run = run + c[:, -1:]
    for p in range(n_pages):
        o_ref[p] = (cums[p] + offsets[p]) - run


def _past_forget(page_table, logf_rows, l):
    nb, n_pages = page_table.shape
    _, n_pool, heads, page = logf_rows.shape
    grid_spec = pltpu.PrefetchScalarGridSpec(
        num_scalar_prefetch=1,
        grid=(nb,),
        in_specs=[pl.BlockSpec((None, n_pool, heads, page), lambda b, pt: (l, 0, 0, 0),
                               pipeline_mode=pl.Buffered(1))],
        out_specs=pl.BlockSpec((None, n_pages, heads, page), lambda b, pt: (b, 0, 0, 0)),
    )
    return pl.pallas_call(
        functools.partial(_past_forget_body, n_pages=n_pages),
        grid_spec=grid_spec,
        out_shape=jax.ShapeDtypeStruct((nb, n_pages, heads, page), F32),
        compiler_params=_params(("arbitrary",)),
        name="past_forget",
    )(page_table, logf_rows)


def _rope_tables(pos):
    half = DIFF_HEAD_DIM // 2
    inv = ROPE_THETA ** (-jnp.arange(half, dtype=F32) / half)
    ang = pos.astype(F32)[:, None] * inv[None, :]
    cos = jnp.cos(ang)
    sin = jnp.sin(ang)
    reps = WIDTH // DIFF_HEAD_DIM
    return (jnp.tile(jnp.concatenate([cos, cos], axis=-1), (1, reps)),
            jnp.tile(jnp.concatenate([-sin, sin], axis=-1), (1, reps)))


def _block_diag(w):
    depth, nblk, d, e = w.shape
    eye = jnp.eye(nblk, dtype=w.dtype)
    return jnp.einsum("lnde,nm->lndme", w, eye).reshape(depth, nblk * d, nblk * e)


def kernel(x_prompt, x_sample, cache_diff_k, cache_diff_v, cache_fox_k, cache_fox_v, cache_fox_logf, state_lru_h, state_conv, page_table, norm_ffn1, w_ffn1_in, w_ffn1_out, norm_mix, w_in, b_forget, diff_q_norm, diff_k_norm, diff_lambda_q1, diff_lambda_k1, diff_lambda_q2, diff_lambda_k2, diff_subln, fox_q_norm, fox_k_norm, conv_w, conv_b, lru_w_a, lru_b_a, lru_w_x, lru_b_x, lru_lambda, w_diff_out, w_fox_out, w_lru_out, w_o, norm_ffn2, w_ffn2_in, w_ffn2_out):
    depth = w_in.shape[0]
    bp, n_t, _ = x_prompt.shape
    nb, n_new, _ = x_sample.shape
    page = cache_diff_k.shape[2]
    past_len = page_table.shape[1] * page

    vec = lambda a: a[:, None, :]
    per_head = lambda a: jnp.tile(a, (1, WIDTH // a.shape[-1]))[:, None, :]
    c0 = 6 * WIDTH
    c1 = c0 + FOX_HEADS
    c2 = c1 + 2 * WIDTH
    W = {
        "norm_ffn1": vec(norm_ffn1), "norm_mix": vec(norm_mix), "norm_ffn2": vec(norm_ffn2),
        "w_ffn1_in": w_ffn1_in.astype(BF16), "w_ffn1_out": w_ffn1_out.astype(BF16),
        "w_ffn2_in": w_ffn2_in.astype(BF16), "w_ffn2_out": w_ffn2_out.astype(BF16),
        "w_qkv": jnp.concatenate(
            [w_in[:, :, :c0], w_in[:, :, c1:c2],
             jnp.pad(w_in[:, :, c0:c1], ((0, 0), (0, 0), (0, LANES - FOX_HEADS)))], axis=-1).astype(BF16),
        "w_gates": w_in[:, :, c2:].astype(BF16),
        "head_sum": jnp.kron(jnp.eye(WIDTH // DIFF_HEAD_DIM, dtype=F32),
                             jnp.ones((DIFF_HEAD_DIM, DIFF_HEAD_DIM), F32)).astype(BF16),
        "g_dq": per_head(diff_q_norm), "g_dk": per_head(diff_k_norm),
        "g_fq": per_head(fox_q_norm), "g_fk": per_head(fox_k_norm),
        "g_subln": vec(diff_subln), "g_subln_col": diff_subln[:, :, None],
        "b_forget": jnp.pad(b_forget, ((0, 0), (0, LANES - FOX_HEADS)))[:, None, :],
        "diff_lambda": jnp.stack([diff_lambda_q1, diff_lambda_k1, diff_lambda_q2, diff_lambda_k2], axis=1),
        "conv_w": conv_w, "conv_b": vec(conv_b),
        "lru_w_ax": jnp.concatenate([_block_diag(lru_w_a), _block_diag(lru_w_x)], axis=-1).astype(BF16),
        "lru_b_a": vec(lru_b_a), "lru_b_x": vec(lru_b_x), "lru_lambda": vec(lru_lambda),
        "w_diff_out": w_diff_out.astype(BF16), "w_fox_out": w_fox_out.astype(BF16),
        "w_lru_out": w_lru_out.astype(BF16), "w_o": w_o.astype(BF16),
    }
    logf_rows = jnp.swapaxes(cache_fox_logf, 2, 3)
    n_pool = cache_diff_k.shape[1]
    diff_k_pages = cache_diff_k.reshape(depth, n_pool, page * DIFF_HEADS, 2 * DIFF_HEAD_DIM)
    diff_v_pages = cache_diff_v.reshape(depth, n_pool, page * DIFF_HEADS, 2 * DIFF_HEAD_DIM)
    fox_k_pages = jnp.transpose(cache_fox_k, (0, 1, 3, 4, 2))
    fox_v_pages = jnp.transpose(cache_fox_v, (0, 1, 3, 4, 2))

    cos_p, sin_p = _rope_tables(jnp.arange(n_t))
    cos_s, sin_s = _rope_tables(jnp.tile(past_len + jnp.arange(n_new), nb))

    tm_p, t_attn, tt_lru = 512, 256, 512
    m_s = nb * n_new
    yp = x_prompt.reshape(bp * n_t, D_MODEL)
    ys = x_sample.reshape(m_s, D_MODEL)
    p_rows, s_rows = [], []
    for l in range(depth):
        lam_init = 0.8 - 0.6 * math.exp(-0.3 * l)

        yp = _ffn(yp, W["norm_ffn1"], W["w_ffn1_in"], W["w_ffn1_out"], l, tm_p)
        dq, dk, dkb, dv, dvb, fq, fk, fkb, fv, fvb, lf, lx, lg = _proj(
            yp, W, l, tm_p, cos_p, sin_p, n_t // tm_p)
        seq = lambda a: a.reshape(bp, n_t, a.shape[-1])
        d_out = _prompt_diff(seq(dq), seq(dkb), seq(dvb), W, l, lam_init, t_attn)
        f_row = _prompt_cum_forget(jnp.swapaxes(seq(lf), 1, 2))
        f_out = _prompt_fox(seq(fq), seq(fkb), seq(fvb), jnp.swapaxes(f_row, 1, 2), f_row, t_attn)
        l_out, h_last = _prompt_lru(seq(lx), seq(lg), W, l, tt_lru)
        flat = lambda a: a.reshape(bp * n_t, WIDTH)
        yp = _merge(yp, flat(d_out), flat(f_out), flat(l_out), W, l, tm_p)
        yp = _ffn(yp, W["norm_ffn2"], W["w_ffn2_in"], W["w_ffn2_out"], l, tm_p)
        p_rows.append((
            dk.reshape(bp, n_t, DIFF_HEADS, 2 * DIFF_HEAD_DIM), dv.reshape(bp, n_t, DIFF_HEADS, 2 * DIFF_HEAD_DIM),
            fk.reshape(bp, n_t, FOX_HEADS, FOX_HEAD_DIM), fv.reshape(bp, n_t, FOX_HEADS, FOX_HEAD_DIM),
            seq(lf), h_last.reshape(bp, WIDTH), seq(lx)[:, n_t - (CONV_WIDTH - 1):, :]))

        ys = _ffn(ys, W["norm_ffn1"], W["w_ffn1_in"], W["w_ffn1_out"], l, m_s)
        dq, dk, dkb, dv, dvb, fq, fk, fkb, fv, fvb, lf, lx, lg = _proj(ys, W, l, m_s, cos_s, sin_s, 1)
        tok = lambda a: a.reshape(nb, n_new, a.shape[-1])
        d_out = _sample_diff(page_table, tok(dq), tok(dk), tok(dv), diff_k_pages, diff_v_pages, W, l, lam_init)
        c_past = _past_forget(page_table, logf_rows, l)
        f_out = _sample_fox(page_table, tok(fq), tok(fk), tok(fv), tok(lf), c_past, fox_k_pages, fox_v_pages, l)
        lo_t, h_last = _sample_lru(jnp.swapaxes(tok(lx), 0, 1), jnp.swapaxes(tok(lg), 0, 1),
                                   jnp.swapaxes(state_conv[l], 0, 1), state_lru_h[l], W, l)
        l_out = jnp.swapaxes(lo_t, 0, 1)
        flat = lambda a: a.reshape(m_s, WIDTH)
        ys = _merge(ys, flat(d_out), flat(f_out), flat(l_out), W, l, m_s)
        ys = _ffn(ys, W["norm_ffn2"], W["w_ffn2_in"], W["w_ffn2_out"], l, m_s)
        conv_last = jnp.concatenate([state_conv[l], tok(lx)], axis=1)[:, n_new:, :]
        s_rows.append((
            dk.reshape(nb, n_new, DIFF_HEADS, 2 * DIFF_HEAD_DIM), dv.reshape(nb, n_new, DIFF_HEADS, 2 * DIFF_HEAD_DIM),
            fk.reshape(nb, n_new, FOX_HEADS, FOX_HEAD_DIM), fv.reshape(nb, n_new, FOX_HEADS, FOX_HEAD_DIM),
            tok(lf), h_last, conv_last))

    stack = lambda rows, i: jnp.stack([r[i] for r in rows])
    return ((yp.reshape(bp, n_t, D_MODEL), ys.reshape(nb, n_new, D_MODEL))
            + tuple(stack(p_rows, i) for i in range(7)) + tuple(stack(s_rows, i) for i in range(7)))
```

```python
import functools
import math

import jax
import jax.numpy as jnp
from jax import lax
from jax.experimental import pallas as pl
from jax.experimental.pallas import tpu as pltpu

F32 = jnp.float32
BF16 = jnp.bfloat16

D_MODEL = 1024
DIFF_HEADS = 4
DIFF_HEAD_DIM = 64
FOX_HEADS = 8
FOX_HEAD_DIM = 64
WIDTH = 512
HEAD_GROUPS = WIDTH // 128
LRU_C = 8.0
CONV_WIDTH = 4
D_FF = 2816
N_BRANCH = 3
ROPE_THETA = 10000.0
EPS = 1e-6
QKV_COLS = 8 * WIDTH + 128

V7X_VMEM_LIMIT = 56 * 1024 * 1024
LANES = 128


def _params(semantics):
    return pltpu.CompilerParams(dimension_semantics=semantics, vmem_limit_bytes=V7X_VMEM_LIMIT)


def _resident(shape, index_map):
    return pl.BlockSpec(shape, index_map, pipeline_mode=pl.Buffered(1))


def _rms(x, g):
    return x * lax.rsqrt(jnp.mean(x * x, axis=-1, keepdims=True) + EPS) * g


def _dot(a, b):
    return jnp.dot(a, b, preferred_element_type=F32)


def _dot_nt(a, b):
    return lax.dot_general(a, b, (((1,), (1,)), ((), ())), preferred_element_type=F32)


def _softplus(x):
    return jnp.maximum(x, 0.0) + jnp.log1p(jnp.exp(-jnp.abs(x)))


def _gelu_tanh(x):
    return 0.5 * x * (1.0 + jnp.tanh(math.sqrt(2.0 / math.pi) * (x + 0.044715 * (x * x * x))))


def _ffn_body(x_ref, g_ref, wg_ref, wu_ref, wo_ref, o_ref):
    x = x_ref[...]
    xb = _rms(x, g_ref[...]).astype(BF16)
    g = _dot(xb, wg_ref[...])
    u = _dot(xb, wu_ref[...])
    act = (g * jax.nn.sigmoid(g) * u).astype(BF16)
    o_ref[...] = x + 0.5 * _dot(act, wo_ref[...])


def _ffn(x, gain, w_in, w_out, l, tm):
    m = x.shape[0]
    return pl.pallas_call(
        _ffn_body,
        grid=(m // tm,),
        in_specs=[
            pl.BlockSpec((tm, D_MODEL), lambda i: (i, 0)),
            _resident((None, 1, D_MODEL), lambda i: (l, 0, 0)),
            _resident((None, D_MODEL, D_FF), lambda i: (l, 0, 0)),
            _resident((None, D_MODEL, D_FF), lambda i: (l, 0, 1)),
            _resident((None, D_FF, D_MODEL), lambda i: (l, 0, 0)),
        ],
        out_specs=pl.BlockSpec((tm, D_MODEL), lambda i: (i, 0)),
        out_shape=jax.ShapeDtypeStruct((m, D_MODEL), F32),
        compiler_params=_params(("parallel",)),
        name="ffn",
    )(x, gain, w_in, w_in, w_out)


def _proj_body(x_ref, g_ref, w_ref, bd_ref, cos_ref, sin_ref, gdq_ref, gdk_ref, gfq_ref, gfk_ref, bf_ref,
               dq_o, dk_o, dkb_o, dv_o, dvb_o, fq_o, fk_o, fkb_o, fv_o, fvb_o, lf_o, lx_o, lg_o):
    tm = x_ref.shape[0]
    xb = _rms(x_ref[...], g_ref[...]).astype(BF16)
    y = _dot(xb, w_ref[...])
    bd = bd_ref[...]

    def head_rms(t, g):
        ss = _dot((t * t).astype(BF16), bd)
        return t * lax.rsqrt(ss * (1.0 / DIFF_HEAD_DIM) + EPS) * g

    lane = lax.broadcasted_iota(jnp.int32, (tm, WIDTH), 1)
    first_half = (lane & (DIFF_HEAD_DIM // 2)) == 0
    cos = cos_ref[...]
    sin = sin_ref[...]

    def rope(t):
        partner = jnp.where(first_half, pltpu.roll(t, WIDTH - DIFF_HEAD_DIM // 2, 1),
                            pltpu.roll(t, DIFF_HEAD_DIM // 2, 1))
        return t * cos + partner * sin

    scale = DIFF_HEAD_DIM ** -0.5
    dq = rope(head_rms(y[:, 0 * WIDTH:1 * WIDTH], gdq_ref[...]))
    dk = rope(head_rms(y[:, 1 * WIDTH:2 * WIDTH], gdk_ref[...]))
    dv = y[:, 2 * WIDTH:3 * WIDTH]
    fq = head_rms(y[:, 3 * WIDTH:4 * WIDTH], gfq_ref[...])
    fk = head_rms(y[:, 4 * WIDTH:5 * WIDTH], gfk_ref[...])
    fv = y[:, 5 * WIDTH:6 * WIDTH]
    z = y[:, 8 * WIDTH:8 * WIDTH + LANES] + bf_ref[...]
    lf = jnp.minimum(z, 0.0) - jnp.log1p(jnp.exp(-jnp.abs(z)))

    dq_o[...] = (dq * scale).astype(BF16)
    dk_o[...] = dk
    dkb_o[...] = dk.astype(BF16)
    dv_o[...] = dv
    dvb_o[...] = dv.astype(BF16)
    fq_o[...] = (fq * scale).astype(BF16)
    fk_o[...] = fk
    fkb_o[...] = fk.astype(BF16)
    fv_o[...] = fv
    fvb_o[...] = fv.astype(BF16)
    lf_o[...] = lf[:, :FOX_HEADS]
    lx_o[...] = y[:, 6 * WIDTH:7 * WIDTH]
    lg_o[...] = y[:, 7 * WIDTH:8 * WIDTH]


def _proj(x, W, l, tm, cos, sin, n_pos_blocks):
    m = x.shape[0]
    row = lambda width: pl.BlockSpec((tm, width), lambda i: (i, 0))
    vec = lambda width: _resident((None, 1, width), lambda i: (l, 0, 0))
    pos = pl.BlockSpec((tm, WIDTH), lambda i: (i % n_pos_blocks, 0))
    f32w = jax.ShapeDtypeStruct((m, WIDTH), F32)
    b16w = jax.ShapeDtypeStruct((m, WIDTH), BF16)
    return pl.pallas_call(
        _proj_body,
        grid=(m // tm,),
        in_specs=[
            row(D_MODEL), vec(D_MODEL),
            _resident((None, D_MODEL, QKV_COLS), lambda i: (l, 0, 0)),
            _resident((WIDTH, WIDTH), lambda i: (0, 0)),
            pos, pos, vec(WIDTH), vec(WIDTH), vec(WIDTH), vec(WIDTH), vec(LANES),
        ],
        out_specs=[row(WIDTH)] * 10 + [row(FOX_HEADS), row(WIDTH), row(WIDTH)],
        out_shape=[b16w, f32w, b16w, f32w, b16w, b16w, f32w, b16w, f32w, b16w,
                   jax.ShapeDtypeStruct((m, FOX_HEADS), F32), f32w, f32w],
        compiler_params=_params(("parallel",)),
        name="proj",
    )(x, W["norm_mix"], W["w_qkv"], W["head_sum"], cos, sin,
      W["g_dq"], W["g_dk"], W["g_fq"], W["g_fk"], W["b_forget"])


def _merge_body(x_ref, g_ref, wg_ref, d_ref, f_ref, r_ref, wd_ref, wf_ref, wl_ref, wo_ref, o_ref):
    x = x_ref[...]
    xb = _rms(x, g_ref[...]).astype(BF16)
    gates = jax.nn.sigmoid(_dot(xb, wg_ref[...]))
    merged = (gates[:, 0:D_MODEL] * _dot(d_ref[...], wd_ref[...])
              + gates[:, D_MODEL:2 * D_MODEL] * _dot(f_ref[...], wf_ref[...])
              + gates[:, 2 * D_MODEL:] * _dot(r_ref[...], wl_ref[...]))
    o_ref[...] = x + _dot(merged.astype(BF16), wo_ref[...])


def _merge(x, d_out, f_out, l_out, W, l, tm):
    m = x.shape[0]
    row = lambda width: pl.BlockSpec((tm, width), lambda i: (i, 0))
    mat = lambda r, c: _resident((None, r, c), lambda i: (l, 0, 0))
    return pl.pallas_call(
        _merge_body,
        grid=(m // tm,),
        in_specs=[row(D_MODEL), mat(1, D_MODEL), mat(D_MODEL, N_BRANCH * D_MODEL),
                  row(WIDTH), row(WIDTH), row(WIDTH),
                  mat(WIDTH, D_MODEL), mat(WIDTH, D_MODEL), mat(WIDTH, D_MODEL), mat(D_MODEL, D_MODEL)],
        out_specs=row(D_MODEL),
        out_shape=jax.ShapeDtypeStruct((m, D_MODEL), F32),
        compiler_params=_params(("parallel",)),
        name="merge",
    )(x, W["norm_mix"], W["w_gates"], d_out, f_out, l_out,
      W["w_diff_out"], W["w_fox_out"], W["w_lru_out"], W["w_o"])


def _stack_halves(qg):
    lane = lax.broadcasted_iota(jnp.int32, qg.shape, 1)
    zero = jnp.zeros_like(qg)
    return jnp.concatenate([jnp.where(lane < 64, qg, zero), jnp.where(lane >= 64, qg, zero)], axis=0)


def _flash_groups(qqs, k_ref, vt_ref, n_full, t, bias_fns):
    cols = 2 * t
    key = lax.broadcasted_iota(jnp.int32, (t, cols), 0)
    qry = lax.broadcasted_iota(jnp.int32, (t, cols), 1)
    causal = key <= jnp.where(qry >= t, qry - t, qry)

    def step(j, carry, masked):
        start = pl.multiple_of(j * t, t)
        out = []
        for grp, (m, l, acc) in enumerate(carry):
            lanes = slice(grp * 128, (grp + 1) * 128)
            s = _dot_nt(k_ref[pl.ds(start, t), lanes], qqs[grp])
            if bias_fns is not None:
                s = bias_fns[grp](s, start)
            if masked:
                s = jnp.where(causal, s, -jnp.inf)
            m_new = jnp.maximum(m, jnp.max(s, axis=0, keepdims=True))
            alpha = jnp.exp(m - m_new)
            p = jnp.exp(s - m_new)
            l = alpha * l + jnp.sum(p, axis=0, keepdims=True)
            acc = alpha * acc + _dot(vt_ref[j, lanes, :], p.astype(BF16))
            out.append((m_new, l, acc))
        return tuple(out)

    init = tuple((jnp.full((1, cols), -jnp.inf, F32), jnp.zeros((1, cols), F32), jnp.zeros((128, cols), F32))
                 for _ in qqs)
    carry = lax.fori_loop(0, n_full, lambda j, c: step(j, c, False), init)
    return [(l, acc) for _, l, acc in step(n_full, carry, True)]


def _diff_lambda(lv_ref, lam_init):
    lv = lv_ref[...]
    s1 = jnp.sum(lv[0:1] * lv[1:2], axis=-1, keepdims=True)
    s2 = jnp.sum(lv[2:3] * lv[3:4], axis=-1, keepdims=True)
    return jnp.exp(s1) - jnp.exp(s2) + lam_init


def _pdiff_body(lv_ref, g_ref, q_ref, k_ref, vt_ref, o_ref, *, t, lam_init):
    i = pl.program_id(1)
    lam = _diff_lambda(lv_ref, lam_init)
    qqs = [_stack_halves(q_ref[:, h * 128:(h + 1) * 128]) for h in range(DIFF_HEADS)]
    for h, (l, acc) in enumerate(_flash_groups(qqs, k_ref, vt_ref, i, t, None)):
        o = acc[:, :t] / l[:, :t] - lam * (acc[:, t:] / l[:, t:])
        o = o * lax.rsqrt(jnp.mean(o * o, axis=0, keepdims=True) + EPS) * g_ref[...]
        o_ref[:, h * 128:(h + 1) * 128] = (o * (1.0 - lam_init)).T.astype(BF16)


def _kv_tiles_t(v, t):
    b, n_t, w = v.shape
    return jnp.swapaxes(v.reshape(b, n_t // t, t, w), 2, 3)


def _prompt_diff(q, k, v, W, l, lam_init, t):
    b, n_t, _ = q.shape
    seq = pl.BlockSpec((None, n_t, WIDTH), lambda bi, i: (bi, 0, 0))
    seq_t = pl.BlockSpec((None, n_t // t, WIDTH, t), lambda bi, i: (bi, 0, 0, 0))
    tile = pl.BlockSpec((None, t, WIDTH), lambda bi, i: (bi, i, 0))
    return pl.pallas_call(
        functools.partial(_pdiff_body, t=t, lam_init=lam_init),
        grid=(b, n_t // t),
        in_specs=[pl.BlockSpec((None, 4, DIFF_HEAD_DIM), lambda bi, i: (l, 0, 0)),
                  pl.BlockSpec((None, 128, 1), lambda bi, i: (l, 0, 0)),
                  tile, seq, seq_t],
        out_specs=tile,
        out_shape=jax.ShapeDtypeStruct((b, n_t, WIDTH), BF16),
        compiler_params=_params(("parallel", "arbitrary")),
        name="prompt_diff_attn",
    )(W["diff_lambda"], W["g_subln_col"], q, k, _kv_tiles_t(v, t))


def _pfox_body(q_ref, k_ref, vt_ref, fq_ref, fk_ref, o_ref, *, t):
    i = pl.program_id(1)
    dim = lax.broadcasted_iota(jnp.int32, (128, t), 0)
    qqs = [_stack_halves(q_ref[:, grp * 128:(grp + 1) * 128]) for grp in range(HEAD_GROUPS)]

    def make_bias(grp):
        fq0 = fq_ref[2 * grp, pl.ds(i, 1), :]
        fq1 = fq_ref[2 * grp + 1, pl.ds(i, 1), :]

        def bias(s, start):
            fk0 = fk_ref[pl.ds(start, t), 2 * grp:2 * grp + 1]
            fk1 = fk_ref[pl.ds(start, t), 2 * grp + 1:2 * grp + 2]
            return jnp.concatenate([s[:, :t] + (fq0 - fk0), s[:, t:] + (fq1 - fk1)], axis=1)

        return bias

    biases = [make_bias(grp) for grp in range(HEAD_GROUPS)]
    for grp, (l, acc) in enumerate(_flash_groups(qqs, k_ref, vt_ref, i, t, biases)):
        o = jnp.where(dim < 64, acc[:, :t] / l[:, :t], acc[:, t:] / l[:, t:])
        o_ref[:, grp * 128:(grp + 1) * 128] = o.T.astype(BF16)


def _prompt_fox(q, k, v, f_col, f_row, t):
    b, n_t, _ = q.shape
    seq = pl.BlockSpec((None, n_t, WIDTH), lambda bi, i: (bi, 0, 0))
    seq_t = pl.BlockSpec((None, n_t // t, WIDTH, t), lambda bi, i: (bi, 0, 0, 0))
    tile = pl.BlockSpec((None, t, WIDTH), lambda bi, i: (bi, i, 0))
    return pl.pallas_call(
        functools.partial(_pfox_body, t=t),
        grid=(b, n_t // t),
        in_specs=[tile, seq, seq_t,
                  pl.BlockSpec((None, FOX_HEADS, n_t // t, t), lambda bi, i: (bi, 0, 0, 0)),
                  pl.BlockSpec((None, n_t, FOX_HEADS), lambda bi, i: (bi, 0, 0))],
        out_specs=tile,
        out_shape=jax.ShapeDtypeStruct((b, n_t, WIDTH), BF16),
        compiler_params=_params(("parallel", "arbitrary")),
        name="prompt_fox_attn",
    )(q, k, _kv_tiles_t(v, t), f_row.reshape(b, FOX_HEADS, n_t // t, t), f_col)


def _lane_cumsum(x):
    n = x.shape[-1]
    lane = lax.broadcasted_iota(jnp.int32, x.shape, x.ndim - 1)
    s = 1
    while s < n:
        x = x + jnp.where(lane >= s, pltpu.roll(x, s, x.ndim - 1), 0.0)
        s *= 2
    return x


def _cumsum_body(x_ref, o_ref):
    o_ref[...] = _lane_cumsum(x_ref[...])


def _prompt_cum_forget(lf_rows):
    b, h, n_t = lf_rows.shape
    spec = pl.BlockSpec((None, h, n_t), lambda bi: (bi, 0, 0))
    return pl.pallas_call(
        _cumsum_body, grid=(b,), in_specs=[spec], out_specs=spec,
        out_shape=jax.ShapeDtypeStruct((b, h, n_t), F32),
        compiler_params=_params(("parallel",)), name="cum_forget",
    )(lf_rows)


def _lru_gates(xc, wax_ref, ba_ref, bx_ref, lam_ref):
    ga = _dot(xc.astype(BF16), wax_ref[...])
    r = jax.nn.sigmoid(ga[:, :WIDTH] + ba_ref[...])
    i = jax.nn.sigmoid(ga[:, WIDTH:] + bx_ref[...])
    log_a = -LRU_C * r * _softplus(-lam_ref[...])
    a = jnp.exp(log_a)
    u = jnp.sqrt(-jnp.tanh(log_a) * (a * a + 1.0)) * (i * xc)
    return a, u


def _plru_body(lx_ref, lg_ref, cw_ref, cb_ref, wax_ref, ba_ref, bx_ref, lam_ref, lo_ref, hl_ref,
               xs_ref, h_ref, *, tt):
    @pl.when(pl.program_id(1) == 0)
    def _():
        xs_ref[0:8, :] = jnp.zeros((8, WIDTH), F32)
        h_ref[...] = jnp.zeros((8, WIDTH), F32)

    x = lx_ref[...]
    xs_ref[8:8 + tt, :] = x
    cw = cw_ref[...]
    xc = cb_ref[...] + xs_ref[5:5 + tt, :] * cw[0:1]
    xc = xc + xs_ref[6:6 + tt, :] * cw[1:2]
    xc = xc + xs_ref[7:7 + tt, :] * cw[2:3]
    xc = xc + x * cw[3:4]
    xs_ref[0:8, :] = x[tt - 8:tt]

    a, u = _lru_gates(xc, wax_ref, ba_ref, bx_ref, lam_ref)
    row = lax.broadcasted_iota(jnp.int32, (tt, WIDTH), 0)
    s = 1
    while s < tt:
        keep = row >= s
        u = a * jnp.where(keep, pltpu.roll(u, s, 0), 0.0) + u
        a = a * jnp.where(keep, pltpu.roll(a, s, 0), 1.0)
        s *= 2
    hs = a * h_ref[0:1, :] + u
    h_ref[0:1, :] = hs[tt - 1:tt]
    hl_ref[...] = hs[tt - 1:tt]
    lo_ref[...] = (hs * _gelu_tanh(lg_ref[...])).astype(BF16)


def _prompt_lru(lx, lg, W, l, tt):
    b, n_t, _ = lx.shape
    tile = pl.BlockSpec((None, tt, WIDTH), lambda bi, i: (bi, i, 0))
    mat = lambda r, c: _resident((None, r, c), lambda bi, i: (l, 0, 0))
    return pl.pallas_call(
        functools.partial(_plru_body, tt=tt),
        grid=(b, n_t // tt),
        in_specs=[tile, tile, mat(CONV_WIDTH, WIDTH), mat(1, WIDTH), mat(WIDTH, 2 * WIDTH),
                  mat(1, WIDTH), mat(1, WIDTH), mat(1, WIDTH)],
        out_specs=[tile, pl.BlockSpec((None, 1, WIDTH), lambda bi, i: (bi, 0, 0))],
        out_shape=[jax.ShapeDtypeStruct((b, n_t, WIDTH), BF16), jax.ShapeDtypeStruct((b, 1, WIDTH), F32)],
        scratch_shapes=[pltpu.VMEM((tt + 8, WIDTH), F32), pltpu.VMEM((8, WIDTH), F32)],
        compiler_params=_params(("parallel", "arbitrary")),
        name="prompt_lru",
    )(lx, lg, W["conv_w"], W["conv_b"], W["lru_w_ax"], W["lru_b_a"], W["lru_b_x"], W["lru_lambda"])


def _slru_body(lx_ref, lg_ref, c0_ref, h0_ref, cw_ref, cb_ref, wax_ref, ba_ref, bx_ref, lam_ref,
               lo_ref, hl_ref, *, n_t, nb):
    cw = cw_ref[...]
    xs = [c0_ref[j] for j in range(CONV_WIDTH - 1)] + [lx_ref[t] for t in range(n_t)]
    xcs = []
    for t in range(n_t):
        xc = cb_ref[...] + xs[t] * cw[0:1]
        for j in range(1, CONV_WIDTH):
            xc = xc + xs[t + j] * cw[j:j + 1]
        xcs.append(xc)
    a, u = _lru_gates(jnp.concatenate(xcs, axis=0), wax_ref, ba_ref, bx_ref, lam_ref)
    h = h0_ref[...]
    for t in range(n_t):
        h = a[t * nb:(t + 1) * nb] * h + u[t * nb:(t + 1) * nb]
        lo_ref[t] = (h * _gelu_tanh(lg_ref[t])).astype(BF16)
    hl_ref[...] = h


def _sample_lru(lx_t, lg_t, conv0_t, h0, W, l):
    n_t, nb, _ = lx_t.shape
    full = lambda *shape: pl.BlockSpec(shape, lambda i: (0,) * len(shape))
    mat = lambda r, c: pl.BlockSpec((None, r, c), lambda i: (l, 0, 0))
    return pl.pallas_call(
        functools.partial(_slru_body, n_t=n_t, nb=nb),
        grid=(1,),
        in_specs=[full(n_t, nb, WIDTH), full(n_t, nb, WIDTH), full(CONV_WIDTH - 1, nb, WIDTH), full(nb, WIDTH),
                  mat(CONV_WIDTH, WIDTH), mat(1, WIDTH), mat(WIDTH, 2 * WIDTH),
                  mat(1, WIDTH), mat(1, WIDTH), mat(1, WIDTH)],
        out_specs=[full(n_t, nb, WIDTH), full(nb, WIDTH)],
        out_shape=[jax.ShapeDtypeStruct((n_t, nb, WIDTH), BF16), jax.ShapeDtypeStruct((nb, WIDTH), F32)],
        compiler_params=_params(("arbitrary",)),
        name="sample_lru",
    )(lx_t, lg_t, conv0_t, h0, W["conv_w"], W["conv_b"], W["lru_w_ax"], W["lru_b_a"], W["lru_b_x"],
      W["lru_lambda"])


PAGES_PER_STEP = 32


def _online_step(state, s, pv):
    m_old, l_old, acc_old = state
    m_new = jnp.maximum(m_old, jnp.max(s, axis=-1, keepdims=True))
    alpha = jnp.exp(m_old - m_new)
    p = jnp.exp(s - m_new)
    return m_new, alpha * l_old + jnp.sum(p, axis=-1, keepdims=True), alpha * acc_old + pv(p)


def _load_state(m_ref, l_ref, acc_ref, h):
    return m_ref[h][:, 0:1], l_ref[h][:, 0:1], acc_ref[h]


def _store_state(m_ref, l_ref, acc_ref, h, state):
    m, l, acc = state
    m_ref[h] = jnp.broadcast_to(m, (8, LANES))
    l_ref[h] = jnp.broadcast_to(l, (8, LANES))
    acc_ref[h] = acc


def _new_keys_state(q8, kn, vn, bias_cols, n_new):
    tok = lax.broadcasted_iota(jnp.int32, (8, 1), 0) % n_new
    cols = []
    for j in range(n_new):
        sj = jnp.sum(q8 * kn[j:j + 1, :], axis=-1, keepdims=True)
        if bias_cols is not None:
            sj = sj + bias_cols[j]
        cols.append(jnp.where(tok >= j, sj, -jnp.inf))
    m = cols[0]
    for sj in cols[1:]:
        m = jnp.maximum(m, sj)
    ps = [jnp.exp(sj - m) for sj in cols]
    l = ps[0]
    acc = ps[0] * vn[0:1, :]
    for j in range(1, n_new):
        l = l + ps[j]
        acc = acc + ps[j] * vn[j:j + 1, :]
    return m, l, acc


def _sdiff_body(pt_ref, lv_ref, g_ref, q_ref, kn_ref, vn_ref, *refs, n_pages, n_new, lam_init):
    k_refs = refs[:n_pages]
    v_refs = refs[n_pages:2 * n_pages]
    o_ref, m_ref, l_ref, acc_ref = refs[2 * n_pages:]
    state_refs = (m_ref, l_ref, acc_ref)
    grp = pl.program_id(1)
    lane = lax.broadcasted_iota(jnp.int32, (8, 128), 1)
    row = lax.broadcasted_iota(jnp.int32, (8, 128), 0)
    q8s = []
    for h in range(DIFF_HEADS):
        qh = q_ref[:, h * 128:(h + 1) * 128].astype(F32)
        q8 = jnp.concatenate([qh, qh], axis=0)
        q8s.append(jnp.where((lane < 64) == (row < n_new), q8, 0.0))

    @pl.when(grp == 0)
    def _():
        for h in range(DIFF_HEADS):
            lanes = slice(h * 128, (h + 1) * 128)
            _store_state(*state_refs, h,
                         _new_keys_state(q8s[h], kn_ref[:, lanes], vn_ref[:, lanes], None, n_new))

    page = k_refs[0].shape[0] // DIFF_HEADS
    old = [_load_state(*state_refs, h) for h in range(DIFF_HEADS)]
    new = []
    for h in range(DIFF_HEADS):
        head_rows = pl.ds(h, page, stride=DIFF_HEADS)
        kcat = jnp.concatenate([r[head_rows, :] for r in k_refs], axis=0)
        vcat = jnp.concatenate([r[head_rows, :] for r in v_refs], axis=0)
        new.append(_online_step(old[h], _dot_nt(q8s[h], kcat), lambda p, vcat=vcat: _dot(p, vcat)))
    for h in range(DIFF_HEADS):
        _store_state(*state_refs, h, new[h])

    @pl.when(grp == pl.num_programs(1) - 1)
    def _():
        lam = _diff_lambda(lv_ref, lam_init)
        for h in range(DIFF_HEADS):
            o = acc_ref[h] / l_ref[h]
            o = o[0:n_new] - lam * o[n_new:2 * n_new]
            o_ref[:, h * 128:(h + 1) * 128] = (_rms(o, g_ref[...]) * (1.0 - lam_init)).astype(BF16)


def _paged_specs(cache, l, n_pages):
    tail = cache.shape[2:]
    zeros = (0,) * len(tail)
    return [pl.BlockSpec((None, None) + tail,
                         lambda b, g, pt, i=i: (l, pt[b, g * n_pages + i]) + zeros)
            for i in range(n_pages)]


def _sample_diff(page_table, q, k_new, v_new, cache_k, cache_v, W, l, lam_init):
    nb, n_new, _ = q.shape
    n_groups = page_table.shape[1] // PAGES_PER_STEP
    per_b = pl.BlockSpec((None, n_new, WIDTH), lambda b, g, pt: (b, 0, 0))
    grid_spec = pltpu.PrefetchScalarGridSpec(
        num_scalar_prefetch=1,
        grid=(nb, n_groups),
        in_specs=[pl.BlockSpec((None, 4, DIFF_HEAD_DIM), lambda b, g, pt: (l, 0, 0)),
                  pl.BlockSpec((None, 1, 128), lambda b, g, pt: (l, 0, 0)),
                  per_b, per_b, per_b]
                 + _paged_specs(cache_k, l, PAGES_PER_STEP) + _paged_specs(cache_v, l, PAGES_PER_STEP),
        out_specs=per_b,
        scratch_shapes=[pltpu.VMEM((DIFF_HEADS, 8, LANES), F32), pltpu.VMEM((DIFF_HEADS, 8, LANES), F32),
                        pltpu.VMEM((DIFF_HEADS, 8, 128), F32)],
    )
    return pl.pallas_call(
        functools.partial(_sdiff_body, n_pages=PAGES_PER_STEP, n_new=n_new, lam_init=lam_init),
        grid_spec=grid_spec,
        out_shape=jax.ShapeDtypeStruct((nb, n_new, WIDTH), BF16),
        compiler_params=_params(("parallel", "arbitrary")),
        name="sample_diff_attn",
    )(page_table, W["diff_lambda"], W["g_subln"], q, k_new, v_new,
      *([cache_k] * PAGES_PER_STEP), *([cache_v] * PAGES_PER_STEP))


def _sfox_body(pt_ref, q_ref, kn_ref, vn_ref, lf_ref, cp_ref, *refs, n_pages, n_new):
    k_refs = refs[:n_pages]
    v_refs = refs[n_pages:2 * n_pages]
    o_ref, m_ref, l_ref, acc_ref = refs[2 * n_pages:]
    state_refs = (m_ref, l_ref, acc_ref)
    grp = pl.program_id(1)
    lf = lf_ref[...]
    f_rows = [lf[0:1]]
    for t in range(1, n_new):
        f_rows.append(f_rows[-1] + lf[t:t + 1])
    f_new = jnp.concatenate(f_rows + f_rows, axis=0)
    q8s = []
    for h in range(FOX_HEADS):
        qh = q_ref[:, h * FOX_HEAD_DIM:(h + 1) * FOX_HEAD_DIM].astype(F32)
        q8s.append(jnp.concatenate([qh, qh], axis=0))

    @pl.when(grp == 0)
    def _():
        for h in range(FOX_HEADS):
            lanes = slice(h * FOX_HEAD_DIM, (h + 1) * FOX_HEAD_DIM)
            bias_cols = [f_new[:, h:h + 1] - f_rows[j][:, h:h + 1] for j in range(n_new)]
            _store_state(*state_refs, h,
                         _new_keys_state(q8s[h], kn_ref[:, lanes], vn_ref[:, lanes], bias_cols, n_new))

    old = [_load_state(*state_refs, h) for h in range(FOX_HEADS)]
    new = []
    for h in range(FOX_HEADS):
        kt = jnp.concatenate([r[h] for r in k_refs], axis=1)
        vt = jnp.concatenate([r[h] for r in v_refs], axis=1)
        fk = jnp.concatenate([cp_ref[i, h:h + 1, :] for i in range(n_pages)], axis=1)
        s = _dot(q8s[h], kt) + (f_new[:, h:h + 1] - fk)
        new.append(_online_step(old[h], s, lambda p, vt=vt: _dot_nt(p, vt)))
    for h in range(FOX_HEADS):
        _store_state(*state_refs, h, new[h])

    @pl.when(grp == pl.num_programs(1) - 1)
    def _():
        for h in range(FOX_HEADS):
            o = acc_ref[h] / l_ref[h][:, 0:FOX_HEAD_DIM]
            o_ref[:, h * FOX_HEAD_DIM:(h + 1) * FOX_HEAD_DIM] = o[0:n_new].astype(BF16)


def _sample_fox(page_table, q, k_new, v_new, lf_new, c_past, cache_k, cache_v, l):
    nb, n_new, _ = q.shape
    n_groups = page_table.shape[1] // PAGES_PER_STEP
    page = c_past.shape[-1]
    per_b = pl.BlockSpec((None, n_new, WIDTH), lambda b, g, pt: (b, 0, 0))
    grid_spec = pltpu.PrefetchScalarGridSpec(
        num_scalar_prefetch=1,
        grid=(nb, n_groups),
        in_specs=[per_b, per_b, per_b,
                  pl.BlockSpec((None, n_new, FOX_HEADS), lambda b, g, pt: (b, 0, 0)),
                  pl.BlockSpec((None, PAGES_PER_STEP, FOX_HEADS, page), lambda b, g, pt: (b, g, 0, 0))]
                 + _paged_specs(cache_k, l, PAGES_PER_STEP) + _paged_specs(cache_v, l, PAGES_PER_STEP),
        out_specs=per_b,
        scratch_shapes=[pltpu.VMEM((FOX_HEADS, 8, LANES), F32), pltpu.VMEM((FOX_HEADS, 8, LANES), F32),
                        pltpu.VMEM((FOX_HEADS, 8, FOX_HEAD_DIM), F32)],
    )
    return pl.pallas_call(
        functools.partial(_sfox_body, n_pages=PAGES_PER_STEP, n_new=n_new),
        grid_spec=grid_spec,
        out_shape=jax.ShapeDtypeStruct((nb, n_new, WIDTH), BF16),
        compiler_params=_params(("parallel", "arbitrary")),
        name="sample_fox_attn",
    )(page_table, q, k_new, v_new, lf_new, c_past,
      *([cache_k] * PAGES_PER_STEP), *([cache_v] * PAGES_PER_STEP))


def _past_forget_body(pt_ref, lf_ref, o_ref, *, n_pages):
    b = pl.program_id(0)
    cums = [_lane_cumsum(lf_ref[pt_ref[b, p]]) for p in range(n_pages)]
    run = jnp.zeros((cums[0].shape[0], 1), F32)
    offsets = []
    for c in cums:
        offsets.append(run)
        run = run + c[:, -1:]
    for p in range(n_pages):
        o_ref[p] = (cums[p] + offsets[p]) - run


def _past_forget(page_table, logf_rows, l):
    nb, n_pages = page_table.shape
    _, n_pool, heads, page = logf_rows.shape
    grid_spec = pltpu.PrefetchScalarGridSpec(
        num_scalar_prefetch=1,
        grid=(nb,),
        in_specs=[pl.BlockSpec((None, n_pool, heads, page), lambda b, pt: (l, 0, 0, 0),
                               pipeline_mode=pl.Buffered(1))],
        out_specs=pl.BlockSpec((None, n_pages, heads, page), lambda b, pt: (b, 0, 0, 0)),
    )
    return pl.pallas_call(
        functools.partial(_past_forget_body, n_pages=n_pages),
        grid_spec=grid_spec,
        out_shape=jax.ShapeDtypeStruct((nb, n_pages, heads, page), F32),
        compiler_params=_params(("arbitrary",)),
        name="past_forget",
    )(page_table, logf_rows)


def _rope_tables(pos):
    half = DIFF_HEAD_DIM // 2
    inv = ROPE_THETA ** (-jnp.arange(half, dtype=F32) / half)
    ang = pos.astype(F32)[:, None] * inv[None, :]
    cos = jnp.cos(ang)
    sin = jnp.sin(ang)
    reps = WIDTH // DIFF_HEAD_DIM
    return (jnp.tile(jnp.concatenate([cos, cos], axis=-1), (1, reps)),
            jnp.tile(jnp.concatenate([-sin, sin], axis=-1), (1, reps)))


def _block_diag(w):
    depth, nblk, d, e = w.shape
    eye = jnp.eye(nblk, dtype=w.dtype)
    return jnp.einsum("lnde,nm->lndme", w, eye).reshape(depth, nblk * d, nblk * e)


def kernel(x_prompt, x_sample, cache_diff_k, cache_diff_v, cache_fox_k, cache_fox_v, cache_fox_logf, state_lru_h, state_conv, page_table, norm_ffn1, w_ffn1_in, w_ffn1_out, norm_mix, w_in, b_forget, diff_q_norm, diff_k_norm, diff_lambda_q1, diff_lambda_k1, diff_lambda_q2, diff_lambda_k2, diff_subln, fox_q_norm, fox_k_norm, conv_w, conv_b, lru_w_a, lru_b_a, lru_w_x, lru_b_x, lru_lambda, w_diff_out, w_fox_out, w_lru_out, w_o, norm_ffn2, w_ffn2_in, w_ffn2_out):
    depth = w_in.shape[0]
    bp, n_t, _ = x_prompt.shape
    nb, n_new, _ = x_sample.shape
    page = cache_diff_k.shape[2]
    past_len = page_table.shape[1] * page

    vec = lambda a: a[:, None, :]
    per_head = lambda a: jnp.tile(a, (1, WIDTH // a.shape[-1]))[:, None, :]
    c0 = 6 * WIDTH
    c1 = c0 + FOX_HEADS
    c2 = c1 + 2 * WIDTH
    W = {
        "norm_ffn1": vec(norm_ffn1), "norm_mix": vec(norm_mix), "norm_ffn2": vec(norm_ffn2),
        "w_ffn1_in": w_ffn1_in.astype(BF16), "w_ffn1_out": w_ffn1_out.astype(BF16),
        "w_ffn2_in": w_ffn2_in.astype(BF16), "w_ffn2_out": w_ffn2_out.astype(BF16),
        "w_qkv": jnp.concatenate(
            [w_in[:, :, :c0], w_in[:, :, c1:c2],
             jnp.pad(w_in[:, :, c0:c1], ((0, 0), (0, 0), (0, LANES - FOX_HEADS)))], axis=-1).astype(BF16),
        "w_gates": w_in[:, :, c2:].astype(BF16),
        "head_sum": jnp.kron(jnp.eye(WIDTH // DIFF_HEAD_DIM, dtype=F32),
                             jnp.ones((DIFF_HEAD_DIM, DIFF_HEAD_DIM), F32)).astype(BF16),
        "g_dq": per_head(diff_q_norm), "g_dk": per_head(diff_k_norm),
        "g_fq": per_head(fox_q_norm), "g_fk": per_head(fox_k_norm),
        "g_subln": vec(diff_subln), "g_subln_col": diff_subln[:, :, None],
        "b_forget": jnp.pad(b_forget, ((0, 0), (0, LANES - FOX_HEADS)))[:, None, :],
        "diff_lambda": jnp.stack([diff_lambda_q1, diff_lambda_k1, diff_lambda_q2, diff_lambda_k2], axis=1),
        "conv_w": conv_w, "conv_b": vec(conv_b),
        "lru_w_ax": jnp.concatenate([_block_diag(lru_w_a), _block_diag(lru_w_x)], axis=-1).astype(BF16),
        "lru_b_a": vec(lru_b_a), "lru_b_x": vec(lru_b_x), "lru_lambda": vec(lru_lambda),
        "w_diff_out": w_diff_out.astype(BF16), "w_fox_out": w_fox_out.astype(BF16),
        "w_lru_out": w_lru_out.astype(BF16), "w_o": w_o.astype(BF16),
    }
    logf_rows = jnp.swapaxes(cache_fox_logf, 2, 3)
    n_pool = cache_diff_k.shape[1]
    diff_k_pages = cache_diff_k.reshape(depth, n_pool, page * DIFF_HEADS, 2 * DIFF_HEAD_DIM)
    diff_v_pages = cache_diff_v.reshape(depth, n_pool, page * DIFF_HEADS, 2 * DIFF_HEAD_DIM)
    fox_k_pages = jnp.transpose(cache_fox_k, (0, 1, 3, 4, 2))
    fox_v_pages = jnp.transpose(cache_fox_v, (0, 1, 3, 4, 2))

    cos_p, sin_p = _rope_tables(jnp.arange(n_t))
    cos_s, sin_s = _rope_tables(jnp.tile(past_len + jnp.arange(n_new), nb))

    tm_p, t_attn, tt_lru = 512, 512, 512
    m_s = nb * n_new
    yp = x_prompt.reshape(bp * n_t, D_MODEL)
    ys = x_sample.reshape(m_s, D_MODEL)
    p_rows, s_rows = [], []
    for l in range(depth):
        lam_init = 0.8 - 0.6 * math.exp(-0.3 * l)

        yp = _ffn(yp, W["norm_ffn1"], W["w_ffn1_in"], W["w_ffn1_out"], l, tm_p)
        dq, dk, dkb, dv, dvb, fq, fk, fkb, fv, fvb, lf, lx, lg = _proj(
            yp, W, l, tm_p, cos_p, sin_p, n_t // tm_p)
        seq = lambda a: a.reshape(bp, n_t, a.shape[-1])
        d_out = _prompt_diff(seq(dq), seq(dkb), seq(dvb), W, l, lam_init, t_attn)
        f_row = _prompt_cum_forget(jnp.swapaxes(seq(lf), 1, 2))
        f_out = _prompt_fox(seq(fq), seq(fkb), seq(fvb), jnp.swapaxes(f_row, 1, 2), f_row, t_attn)
        l_out, h_last = _prompt_lru(seq(lx), seq(lg), W, l, tt_lru)
        flat = lambda a: a.reshape(bp * n_t, WIDTH)
        yp = _merge(yp, flat(d_out), flat(f_out), flat(l_out), W, l, tm_p)
        yp = _ffn(yp, W["norm_ffn2"], W["w_ffn2_in"], W["w_ffn2_out"], l, tm_p)
        p_rows.append((
            dk.reshape(bp, n_t, DIFF_HEADS, 2 * DIFF_HEAD_DIM), dv.reshape(bp, n_t, DIFF_HEADS, 2 * DIFF_HEAD_DIM),
            fk.reshape(bp, n_t, FOX_HEADS, FOX_HEAD_DIM), fv.reshape(bp, n_t, FOX_HEADS, FOX_HEAD_DIM),
            seq(lf), h_last.reshape(bp, WIDTH), seq(lx)[:, n_t - (CONV_WIDTH - 1):, :]))

        ys = _ffn(ys, W["norm_ffn1"], W["w_ffn1_in"], W["w_ffn1_out"], l, m_s)
        dq, dk, dkb, dv, dvb, fq, fk, fkb, fv, fvb, lf, lx, lg = _proj(ys, W, l, m_s, cos_s, sin_s, 1)
        tok = lambda a: a.reshape(nb, n_new, a.shape[-1])
        d_out = _sample_diff(page_table, tok(dq), tok(dk), tok(dv), diff_k_pages, diff_v_pages, W, l, lam_init)
        c_past = _past_forget(page_table, logf_rows, l)
        f_out = _sample_fox(page_table, tok(fq), tok(fk), tok(fv), tok(lf), c_past, fox_k_pages, fox_v_pages, l)
        lo_t, h_last = _sample_lru(jnp.swapaxes(tok(lx), 0, 1), jnp.swapaxes(tok(lg), 0, 1),
                                   jnp.swapaxes(state_conv[l], 0, 1), state_lru_h[l], W, l)
        l_out = jnp.swapaxes(lo_t, 0, 1)
        flat = lambda a: a.reshape(m_s, WIDTH)
        ys = _merge(ys, flat(d_out), flat(f_out), flat(l_out), W, l, m_s)
        ys = _ffn(ys, W["norm_ffn2"], W["w_ffn2_in"], W["w_ffn2_out"], l, m_s)
        conv_last = jnp.concatenate([state_conv[l], tok(lx)], axis=1)[:, n_new:, :]
        s_rows.append((
            dk.reshape(nb, n_new, DIFF_HEADS, 2 * DIFF_HEAD_DIM), dv.reshape(nb, n_new, DIFF_HEADS, 2 * DIFF_HEAD_DIM),
            fk.reshape(nb, n_new, FOX_HEADS, FOX_HEAD_DIM), fv.reshape(nb, n_new, FOX_HEADS, FOX_HEAD_DIM),
            tok(lf), h_last, conv_last))

    stack = lambda rows, i: jnp.stack([r[i] for r in rows])
    return ((yp.reshape(bp, n_t, D_MODEL), ys.reshape(nb, n_new, D_MODEL))
            + tuple(stack(p_rows, i) for i in range(7)) + tuple(stack(s_rows, i) for i in range(7)))
```

```python
import functools
import math

import jax
import jax.numpy as jnp
from jax import lax
from jax.experimental import pallas as pl
from jax.experimental.pallas import tpu as pltpu

F32 = jnp.float32
BF16 = jnp.bfloat16

D_MODEL = 1024
DIFF_HEADS = 4
DIFF_HEAD_DIM = 64
FOX_HEADS = 8
FOX_HEAD_DIM = 64
WIDTH = 512
HEAD_GROUPS = WIDTH // 128
LRU_C = 8.0
CONV_WIDTH = 4
D_FF = 2816
N_BRANCH = 3
ROPE_THETA = 10000.0
EPS = 1e-6
QKV_COLS = 8 * WIDTH + 128

V7X_VMEM_LIMIT = 56 * 1024 * 1024
LANES = 128


def _params(semantics):
    return pltpu.CompilerParams(dimension_semantics=semantics, vmem_limit_bytes=V7X_VMEM_LIMIT)


def _resident(shape, index_map):
    return pl.BlockSpec(shape, index_map, pipeline_mode=pl.Buffered(1))


def _rms(x, g):
    return x * lax.rsqrt(jnp.mean(x * x, axis=-1, keepdims=True) + EPS) * g


def _dot(a, b):
    return jnp.dot(a, b, preferred_element_type=F32)


def _dot_nt(a, b):
    return lax.dot_general(a, b, (((1,), (1,)), ((), ())), preferred_element_type=F32)


def _softplus(x):
    return jnp.maximum(x, 0.0) + jnp.log1p(jnp.exp(-jnp.abs(x)))


def _gelu_tanh(x):
    return 0.5 * x * (1.0 + jnp.tanh(math.sqrt(2.0 / math.pi) * (x + 0.044715 * (x * x * x))))


def _ffn_body(x_ref, g_ref, wg_ref, wu_ref, wo_ref, o_ref):
    x = x_ref[...]
    xb = _rms(x, g_ref[...]).astype(BF16)
    g = _dot(xb, wg_ref[...])
    u = _dot(xb, wu_ref[...])
    act = (g * jax.nn.sigmoid(g) * u).astype(BF16)
    o_ref[...] = x + 0.5 * _dot(act, wo_ref[...])


def _ffn(x, gain, w_in, w_out, l, tm):
    m = x.shape[0]
    return pl.pallas_call(
        _ffn_body,
        grid=(m // tm,),
        in_specs=[
            pl.BlockSpec((tm, D_MODEL), lambda i: (i, 0)),
            _resident((None, 1, D_MODEL), lambda i: (l, 0, 0)),
            _resident((None, D_MODEL, D_FF), lambda i: (l, 0, 0)),
            _resident((None, D_MODEL, D_FF), lambda i: (l, 0, 1)),
            _resident((None, D_FF, D_MODEL), lambda i: (l, 0, 0)),
        ],
        out_specs=pl.BlockSpec((tm, D_MODEL), lambda i: (i, 0)),
        out_shape=jax.ShapeDtypeStruct((m, D_MODEL), F32),
        compiler_params=_params(("parallel",)),
        name="ffn",
    )(x, gain, w_in, w_in, w_out)


N_PROJ_OUT = 13
PROJ_LEAF_OUT = (1, 3, 6, 8)


def _proj_body(x_ref, g_ref, w_ref, bd_ref, cos_ref, sin_ref, gdq_ref, gdk_ref, gfq_ref, gfk_ref, bf_ref, *refs):
    dq_o, dk_o, dkb_o, dv_o, dvb_o, fq_o, fk_o, fkb_o, fv_o, fvb_o, lf_o, lx_o, lg_o = refs[-N_PROJ_OUT:]
    tm = x_ref.shape[0]
    xb = _rms(x_ref[...], g_ref[...]).astype(BF16)
    y = _dot(xb, w_ref[...])
    bd = bd_ref[...]

    def head_rms(t, g):
        ss = _dot((t * t).astype(BF16), bd)
        return t * lax.rsqrt(ss * (1.0 / DIFF_HEAD_DIM) + EPS) * g

    lane = lax.broadcasted_iota(jnp.int32, (tm, WIDTH), 1)
    first_half = (lane & (DIFF_HEAD_DIM // 2)) == 0
    cos = cos_ref[...]
    sin = sin_ref[...]

    def rope(t):
        partner = jnp.where(first_half, pltpu.roll(t, WIDTH - DIFF_HEAD_DIM // 2, 1),
                            pltpu.roll(t, DIFF_HEAD_DIM // 2, 1))
        return t * cos + partner * sin

    scale = DIFF_HEAD_DIM ** -0.5
    dq = rope(head_rms(y[:, 0 * WIDTH:1 * WIDTH], gdq_ref[...]))
    dk = rope(head_rms(y[:, 1 * WIDTH:2 * WIDTH], gdk_ref[...]))
    dv = y[:, 2 * WIDTH:3 * WIDTH]
    fq = head_rms(y[:, 3 * WIDTH:4 * WIDTH], gfq_ref[...])
    fk = head_rms(y[:, 4 * WIDTH:5 * WIDTH], gfk_ref[...])
    fv = y[:, 5 * WIDTH:6 * WIDTH]
    z = y[:, 8 * WIDTH:8 * WIDTH + LANES] + bf_ref[...]
    lf = jnp.minimum(z, 0.0) - jnp.log1p(jnp.exp(-jnp.abs(z)))

    dq_o[...] = (dq * scale).astype(BF16)
    dk_o[...] = dk
    dkb_o[...] = dk.astype(BF16)
    dv_o[...] = dv
    dvb_o[...] = dv.astype(BF16)
    fq_o[...] = (fq * scale).astype(BF16)
    fk_o[...] = fk
    fkb_o[...] = fk.astype(BF16)
    fv_o[...] = fv
    fvb_o[...] = fv.astype(BF16)
    lf_o[...] = lf[:, :FOX_HEADS]
    lx_o[...] = y[:, 6 * WIDTH:7 * WIDTH]
    lg_o[...] = y[:, 7 * WIDTH:8 * WIDTH]


def _proj(x, W, l, depth, tm, cos, sin, n_pos_blocks, leaves):
    m = x.shape[0]
    row = lambda width: pl.BlockSpec((tm, width), lambda i: (i, 0))
    vec = lambda width: _resident((None, 1, width), lambda i: (l, 0, 0))
    pos = pl.BlockSpec((tm, WIDTH), lambda i: (i % n_pos_blocks, 0))
    slot = pl.BlockSpec((None, tm, WIDTH), lambda i: (l, i, 0))
    f32w = jax.ShapeDtypeStruct((m, WIDTH), F32)
    b16w = jax.ShapeDtypeStruct((m, WIDTH), BF16)
    stacked = jax.ShapeDtypeStruct((depth, m, WIDTH), F32)
    out_specs = [row(WIDTH)] * 10 + [row(FOX_HEADS), row(WIDTH), row(WIDTH)]
    out_shape = [b16w, f32w, b16w, f32w, b16w, b16w, f32w, b16w, f32w, b16w,
                 jax.ShapeDtypeStruct((m, FOX_HEADS), F32), f32w, f32w]
    for o in PROJ_LEAF_OUT:
        out_specs[o] = slot
        out_shape[o] = stacked
    in_specs = [
        row(D_MODEL), vec(D_MODEL),
        _resident((None, D_MODEL, QKV_COLS), lambda i: (l, 0, 0)),
        _resident((WIDTH, WIDTH), lambda i: (0, 0)),
        pos, pos, vec(WIDTH), vec(WIDTH), vec(WIDTH), vec(WIDTH), vec(LANES),
    ]
    args = [x, W["norm_mix"], W["w_qkv"], W["head_sum"], cos, sin,
            W["g_dq"], W["g_dk"], W["g_fq"], W["g_fk"], W["b_forget"]]
    aliases = {}
    if leaves is not None:
        aliases = {len(args) + n: o for n, o in enumerate(PROJ_LEAF_OUT)}
        in_specs += [pl.BlockSpec(memory_space=pl.ANY)] * len(leaves)
        args += list(leaves)
    return pl.pallas_call(
        _proj_body,
        grid=(m // tm,),
        in_specs=in_specs,
        out_specs=out_specs,
        out_shape=out_shape,
        input_output_aliases=aliases,
        compiler_params=_params(("parallel",)),
        name="proj",
    )(*args)


def _merge_body(x_ref, g_ref, wg_ref, d_ref, f_ref, r_ref, wd_ref, wf_ref, wl_ref, wo_ref, o_ref):
    x = x_ref[...]
    xb = _rms(x, g_ref[...]).astype(BF16)
    gates = jax.nn.sigmoid(_dot(xb, wg_ref[...]))
    merged = (gates[:, 0:D_MODEL] * _dot(d_ref[...], wd_ref[...])
              + gates[:, D_MODEL:2 * D_MODEL] * _dot(f_ref[...], wf_ref[...])
              + gates[:, 2 * D_MODEL:] * _dot(r_ref[...], wl_ref[...]))
    o_ref[...] = x + _dot(merged.astype(BF16), wo_ref[...])


def _merge(x, d_out, f_out, l_out, W, l, tm):
    m = x.shape[0]
    row = lambda width: pl.BlockSpec((tm, width), lambda i: (i, 0))
    mat = lambda r, c: _resident((None, r, c), lambda i: (l, 0, 0))
    return pl.pallas_call(
        _merge_body,
        grid=(m // tm,),
        in_specs=[row(D_MODEL), mat(1, D_MODEL), mat(D_MODEL, N_BRANCH * D_MODEL),
                  row(WIDTH), row(WIDTH), row(WIDTH),
                  mat(WIDTH, D_MODEL), mat(WIDTH, D_MODEL), mat(WIDTH, D_MODEL), mat(D_MODEL, D_MODEL)],
        out_specs=row(D_MODEL),
        out_shape=jax.ShapeDtypeStruct((m, D_MODEL), F32),
        compiler_params=_params(("parallel",)),
        name="merge",
    )(x, W["norm_mix"], W["w_gates"], d_out, f_out, l_out,
      W["w_diff_out"], W["w_fox_out"], W["w_lru_out"], W["w_o"])


def _stack_halves(qg):
    lane = lax.broadcasted_iota(jnp.int32, qg.shape, 1)
    zero = jnp.zeros_like(qg)
    return jnp.concatenate([jnp.where(lane < 64, qg, zero), jnp.where(lane >= 64, qg, zero)], axis=0)


def _flash_groups(qqs, k_ref, vt_ref, n_full, t, bias_fns):
    cols = 2 * t
    key = lax.broadcasted_iota(jnp.int32, (t, cols), 0)
    qry = lax.broadcasted_iota(jnp.int32, (t, cols), 1)
    causal = key <= jnp.where(qry >= t, qry - t, qry)

    def step(j, carry, masked):
        start = pl.multiple_of(j * t, t)
        out = []
        for grp, (m, l, acc) in enumerate(carry):
            lanes = slice(grp * 128, (grp + 1) * 128)
            s = _dot_nt(k_ref[pl.ds(start, t), lanes], qqs[grp])
            if bias_fns is not None:
                s = bias_fns[grp](s, start)
            if masked:
                s = jnp.where(causal, s, -jnp.inf)
            m_new = jnp.maximum(m, jnp.max(s, axis=0, keepdims=True))
            alpha = jnp.exp(m - m_new)
            p = jnp.exp(s - m_new)
            l = alpha * l + jnp.sum(p, axis=0, keepdims=True)
            acc = alpha * acc + _dot(vt_ref[j, lanes, :], p.astype(BF16))
            out.append((m_new, l, acc))
        return tuple(out)

    init = tuple((jnp.full((1, cols), -jnp.inf, F32), jnp.zeros((1, cols), F32), jnp.zeros((128, cols), F32))
                 for _ in qqs)
    carry = lax.fori_loop(0, n_full, lambda j, c: step(j, c, False), init)
    return [(l, acc) for _, l, acc in step(n_full, carry, True)]


def _diff_lambda(lv_ref, lam_init):
    lv = lv_ref[...]
    s1 = jnp.sum(lv[0:1] * lv[1:2], axis=-1, keepdims=True)
    s2 = jnp.sum(lv[2:3] * lv[3:4], axis=-1, keepdims=True)
    return jnp.exp(s1) - jnp.exp(s2) + lam_init


def _pdiff_body(lv_ref, g_ref, q_ref, k_ref, vt_ref, o_ref, *, t, lam_init):
    i = pl.program_id(1)
    lam = _diff_lambda(lv_ref, lam_init)
    qqs = [_stack_halves(q_ref[:, h * 128:(h + 1) * 128]) for h in range(DIFF_HEADS)]
    for h, (l, acc) in enumerate(_flash_groups(qqs, k_ref, vt_ref, i, t, None)):
        o = acc[:, :t] / l[:, :t] - lam * (acc[:, t:] / l[:, t:])
        o = o * lax.rsqrt(jnp.mean(o * o, axis=0, keepdims=True) + EPS) * g_ref[...]
        o_ref[:, h * 128:(h + 1) * 128] = (o * (1.0 - lam_init)).T.astype(BF16)


def _kv_tiles_t(v, t):
    b, n_t, w = v.shape
    return jnp.swapaxes(v.reshape(b, n_t // t, t, w), 2, 3)


def _prompt_diff(q, k, v, W, l, lam_init, t):
    b, n_t, _ = q.shape
    seq = pl.BlockSpec((None, n_t, WIDTH), lambda bi, i: (bi, 0, 0))
    seq_t = pl.BlockSpec((None, n_t // t, WIDTH, t), lambda bi, i: (bi, 0, 0, 0))
    tile = pl.BlockSpec((None, t, WIDTH), lambda bi, i: (bi, i, 0))
    return pl.pallas_call(
        functools.partial(_pdiff_body, t=t, lam_init=lam_init),
        grid=(b, n_t // t),
        in_specs=[pl.BlockSpec((None, 4, DIFF_HEAD_DIM), lambda bi, i: (l, 0, 0)),
                  pl.BlockSpec((None, 128, 1), lambda bi, i: (l, 0, 0)),
                  tile, seq, seq_t],
        out_specs=tile,
        out_shape=jax.ShapeDtypeStruct((b, n_t, WIDTH), BF16),
        compiler_params=_params(("parallel", "arbitrary")),
        name="prompt_diff_attn",
    )(W["diff_lambda"], W["g_subln_col"], q, k, _kv_tiles_t(v, t))


def _pfox_body(q_ref, k_ref, vt_ref, fq_ref, fk_ref, o_ref, *, t):
    i = pl.program_id(1)
    dim = lax.broadcasted_iota(jnp.int32, (128, t), 0)
    qqs = [_stack_halves(q_ref[:, grp * 128:(grp + 1) * 128]) for grp in range(HEAD_GROUPS)]

    def make_bias(grp):
        fq0 = fq_ref[2 * grp, pl.ds(i, 1), :]
        fq1 = fq_ref[2 * grp + 1, pl.ds(i, 1), :]

        def bias(s, start):
            fk0 = fk_ref[pl.ds(start, t), 2 * grp:2 * grp + 1]
            fk1 = fk_ref[pl.ds(start, t), 2 * grp + 1:2 * grp + 2]
            return jnp.concatenate([s[:, :t] + (fq0 - fk0), s[:, t:] + (fq1 - fk1)], axis=1)

        return bias

    biases = [make_bias(grp) for grp in range(HEAD_GROUPS)]
    for grp, (l, acc) in enumerate(_flash_groups(qqs, k_ref, vt_ref, i, t, biases)):
        o = jnp.where(dim < 64, acc[:, :t] / l[:, :t], acc[:, t:] / l[:, t:])
        o_ref[:, grp * 128:(grp + 1) * 128] = o.T.astype(BF16)


def _prompt_fox(q, k, v, f_col, f_row, t):
    b, n_t, _ = q.shape
    seq = pl.BlockSpec((None, n_t, WIDTH), lambda bi, i: (bi, 0, 0))
    seq_t = pl.BlockSpec((None, n_t // t, WIDTH, t), lambda bi, i: (bi, 0, 0, 0))
    tile = pl.BlockSpec((None, t, WIDTH), lambda bi, i: (bi, i, 0))
    return pl.pallas_call(
        functools.partial(_pfox_body, t=t),
        grid=(b, n_t // t),
        in_specs=[tile, seq, seq_t,
                  pl.BlockSpec((None, FOX_HEADS, n_t // t, t), lambda bi, i: (bi, 0, 0, 0)),
                  pl.BlockSpec((None, n_t, FOX_HEADS), lambda bi, i: (bi, 0, 0))],
        out_specs=tile,
        out_shape=jax.ShapeDtypeStruct((b, n_t, WIDTH), BF16),
        compiler_params=_params(("parallel", "arbitrary")),
        name="prompt_fox_attn",
    )(q, k, _kv_tiles_t(v, t), f_row.reshape(b, FOX_HEADS, n_t // t, t), f_col)


def _lane_cumsum(x):
    n = x.shape[-1]
    lane = lax.broadcasted_iota(jnp.int32, x.shape, x.ndim - 1)
    s = 1
    while s < n:
        x = x + jnp.where(lane >= s, pltpu.roll(x, s, x.ndim - 1), 0.0)
        s *= 2
    return x


def _cumsum_body(x_ref, o_ref):
    o_ref[...] = _lane_cumsum(x_ref[...])


def _prompt_cum_forget(lf_rows):
    b, h, n_t = lf_rows.shape
    spec = pl.BlockSpec((None, h, n_t), lambda bi: (bi, 0, 0))
    return pl.pallas_call(
        _cumsum_body, grid=(b,), in_specs=[spec], out_specs=spec,
        out_shape=jax.ShapeDtypeStruct((b, h, n_t), F32),
        compiler_params=_params(("parallel",)), name="cum_forget",
    )(lf_rows)


def _lru_gates(xc, wax_ref, ba_ref, bx_ref, lam_ref):
    ga = _dot(xc.astype(BF16), wax_ref[...])
    r = jax.nn.sigmoid(ga[:, :WIDTH] + ba_ref[...])
    i = jax.nn.sigmoid(ga[:, WIDTH:] + bx_ref[...])
    log_a = -LRU_C * r * _softplus(-lam_ref[...])
    a = jnp.exp(log_a)
    u = jnp.sqrt(-jnp.tanh(log_a) * (a * a + 1.0)) * (i * xc)
    return a, u


def _plru_body(lx_ref, lg_ref, cw_ref, cb_ref, wax_ref, ba_ref, bx_ref, lam_ref, lo_ref, hl_ref,
               xs_ref, h_ref, *, tt):
    @pl.when(pl.program_id(1) == 0)
    def _():
        xs_ref[0:8, :] = jnp.zeros((8, WIDTH), F32)
        h_ref[...] = jnp.zeros((8, WIDTH), F32)

    x = lx_ref[...]
    xs_ref[8:8 + tt, :] = x
    cw = cw_ref[...]
    xc = cb_ref[...] + xs_ref[5:5 + tt, :] * cw[0:1]
    xc = xc + xs_ref[6:6 + tt, :] * cw[1:2]
    xc = xc + xs_ref[7:7 + tt, :] * cw[2:3]
    xc = xc + x * cw[3:4]
    xs_ref[0:8, :] = x[tt - 8:tt]

    a, u = _lru_gates(xc, wax_ref, ba_ref, bx_ref, lam_ref)
    row = lax.broadcasted_iota(jnp.int32, (tt, WIDTH), 0)
    s = 1
    while s < tt:
        keep = row >= s
        u = a * jnp.where(keep, pltpu.roll(u, s, 0), 0.0) + u
        a = a * jnp.where(keep, pltpu.roll(a, s, 0), 1.0)
        s *= 2
    hs = a * h_ref[0:1, :] + u
    h_ref[0:1, :] = hs[tt - 1:tt]
    hl_ref[...] = hs[tt - 1:tt]
    lo_ref[...] = (hs * _gelu_tanh(lg_ref[...])).astype(BF16)


def _prompt_lru(lx, lg, W, l, tt):
    b, n_t, _ = lx.shape
    tile = pl.BlockSpec((None, tt, WIDTH), lambda bi, i: (bi, i, 0))
    mat = lambda r, c: _resident((None, r, c), lambda bi, i: (l, 0, 0))
    return pl.pallas_call(
        functools.partial(_plru_body, tt=tt),
        grid=(b, n_t // tt),
        in_specs=[tile, tile, mat(CONV_WIDTH, WIDTH), mat(1, WIDTH), mat(WIDTH, 2 * WIDTH),
                  mat(1, WIDTH), mat(1, WIDTH), mat(1, WIDTH)],
        out_specs=[tile, pl.BlockSpec((None, 1, WIDTH), lambda bi, i: (bi, 0, 0))],
        out_shape=[jax.ShapeDtypeStruct((b, n_t, WIDTH), BF16), jax.ShapeDtypeStruct((b, 1, WIDTH), F32)],
        scratch_shapes=[pltpu.VMEM((tt + 8, WIDTH), F32), pltpu.VMEM((8, WIDTH), F32)],
        compiler_params=_params(("parallel", "arbitrary")),
        name="prompt_lru",
    )(lx, lg, W["conv_w"], W["conv_b"], W["lru_w_ax"], W["lru_b_a"], W["lru_b_x"], W["lru_lambda"])


def _slru_body(lx_ref, lg_ref, c0_ref, h0_ref, cw_ref, cb_ref, wax_ref, ba_ref, bx_ref, lam_ref,
               lo_ref, hl_ref, *, n_t, nb):
    cw = cw_ref[...]
    xs = [c0_ref[j] for j in range(CONV_WIDTH - 1)] + [lx_ref[t] for t in range(n_t)]
    xcs = []
    for t in range(n_t):
        xc = cb_ref[...] + xs[t] * cw[0:1]
        for j in range(1, CONV_WIDTH):
            xc = xc + xs[t + j] * cw[j:j + 1]
        xcs.append(xc)
    a, u = _lru_gates(jnp.concatenate(xcs, axis=0), wax_ref, ba_ref, bx_ref, lam_ref)
    h = h0_ref[...]
    for t in range(n_t):
        h = a[t * nb:(t + 1) * nb] * h + u[t * nb:(t + 1) * nb]
        lo_ref[t] = (h * _gelu_tanh(lg_ref[t])).astype(BF16)
    hl_ref[...] = h


def _sample_lru(lx_t, lg_t, conv0_t, h0, W, l):
    n_t, nb, _ = lx_t.shape
    full = lambda *shape: pl.BlockSpec(shape, lambda i: (0,) * len(shape))
    mat = lambda r, c: pl.BlockSpec((None, r, c), lambda i: (l, 0, 0))
    return pl.pallas_call(
        functools.partial(_slru_body, n_t=n_t, nb=nb),
        grid=(1,),
        in_specs=[full(n_t, nb, WIDTH), full(n_t, nb, WIDTH), full(CONV_WIDTH - 1, nb, WIDTH), full(nb, WIDTH),
                  mat(CONV_WIDTH, WIDTH), mat(1, WIDTH), mat(WIDTH, 2 * WIDTH),
                  mat(1, WIDTH), mat(1, WIDTH), mat(1, WIDTH)],
        out_specs=[full(n_t, nb, WIDTH), full(nb, WIDTH)],
        out_shape=[jax.ShapeDtypeStruct((n_t, nb, WIDTH), BF16), jax.ShapeDtypeStruct((nb, WIDTH), F32)],
        compiler_params=_params(("arbitrary",)),
        name="sample_lru",
    )(lx_t, lg_t, conv0_t, h0, W["conv_w"], W["conv_b"], W["lru_w_ax"], W["lru_b_a"], W["lru_b_x"],
      W["lru_lambda"])


PAGES_PER_STEP = 32


def _online_step(state, s, pv):
    m_old, l_old, acc_old = state
    m_new = jnp.maximum(m_old, jnp.max(s, axis=-1, keepdims=True))
    alpha = jnp.exp(m_old - m_new)
    p = jnp.exp(s - m_new)
    return m_new, alpha * l_old + jnp.sum(p, axis=-1, keepdims=True), alpha * acc_old + pv(p)


def _load_state(m_ref, l_ref, acc_ref, h):
    return m_ref[h][:, 0:1], l_ref[h][:, 0:1], acc_ref[h]


def _store_state(m_ref, l_ref, acc_ref, h, state):
    m, l, acc = state
    m_ref[h] = jnp.broadcast_to(m, (8, LANES))
    l_ref[h] = jnp.broadcast_to(l, (8, LANES))
    acc_ref[h] = acc


def _new_keys_state(q8, kn, vn, bias_cols, n_new):
    tok = lax.broadcasted_iota(jnp.int32, (8, 1), 0) % n_new
    cols = []
    for j in range(n_new):
        sj = jnp.sum(q8 * kn[j:j + 1, :], axis=-1, keepdims=True)
        if bias_cols is not None:
            sj = sj + bias_cols[j]
        cols.append(jnp.where(tok >= j, sj, -jnp.inf))
    m = cols[0]
    for sj in cols[1:]:
        m = jnp.maximum(m, sj)
    ps = [jnp.exp(sj - m) for sj in cols]
    l = ps[0]
    acc = ps[0] * vn[0:1, :]
    for j in range(1, n_new):
        l = l + ps[j]
        acc = acc + ps[j] * vn[j:j + 1, :]
    return m, l, acc


def _sdiff_body(pt_ref, lv_ref, g_ref, q_ref, kn_ref, vn_ref, *refs, n_pages, n_new, lam_init):
    k_refs = refs[:n_pages]
    v_refs = refs[n_pages:2 * n_pages]
    o_ref, m_ref, l_ref, acc_ref = refs[2 * n_pages:]
    state_refs = (m_ref, l_ref, acc_ref)
    grp = pl.program_id(1)
    lane = lax.broadcasted_iota(jnp.int32, (8, 128), 1)
    row = lax.broadcasted_iota(jnp.int32, (8, 128), 0)
    q8s = []
    for h in range(DIFF_HEADS):
        qh = q_ref[:, h * 128:(h + 1) * 128].astype(F32)
        q8 = jnp.concatenate([qh, qh], axis=0)
        q8s.append(jnp.where((lane < 64) == (row < n_new), q8, 0.0))

    @pl.when(grp == 0)
    def _():
        for h in range(DIFF_HEADS):
            lanes = slice(h * 128, (h + 1) * 128)
            _store_state(*state_refs, h,
                         _new_keys_state(q8s[h], kn_ref[:, lanes], vn_ref[:, lanes], None, n_new))

    page = k_refs[0].shape[0] // DIFF_HEADS
    old = [_load_state(*state_refs, h) for h in range(DIFF_HEADS)]
    new = []
    for h in range(DIFF_HEADS):
        head_rows = pl.ds(h, page, stride=DIFF_HEADS)
        kcat = jnp.concatenate([r[head_rows, :] for r in k_refs], axis=0)
        vcat = jnp.concatenate([r[head_rows, :] for r in v_refs], axis=0)
        new.append(_online_step(old[h], _dot_nt(q8s[h], kcat), lambda p, vcat=vcat: _dot(p, vcat)))
    for h in range(DIFF_HEADS):
        _store_state(*state_refs, h, new[h])

    @pl.when(grp == pl.num_programs(1) - 1)
    def _():
        lam = _diff_lambda(lv_ref, lam_init)
        for h in range(DIFF_HEADS):
            o = acc_ref[h] / l_ref[h]
            o = o[0:n_new] - lam * o[n_new:2 * n_new]
            o_ref[:, h * 128:(h + 1) * 128] = (_rms(o, g_ref[...]) * (1.0 - lam_init)).astype(BF16)


def _paged_specs(cache, l, n_pages):
    tail = cache.shape[2:]
    zeros = (0,) * len(tail)
    return [pl.BlockSpec((None, None) + tail,
                         lambda b, g, pt, i=i: (l, pt[b, g * n_pages + i]) + zeros)
            for i in range(n_pages)]


def _sample_diff(page_table, q, k_new, v_new, cache_k, cache_v, W, l, lam_init):
    nb, n_new, _ = q.shape
    n_groups = page_table.shape[1] // PAGES_PER_STEP
    per_b = pl.BlockSpec((None, n_new, WIDTH), lambda b, g, pt: (b, 0, 0))
    grid_spec = pltpu.PrefetchScalarGridSpec(
        num_scalar_prefetch=1,
        grid=(nb, n_groups),
        in_specs=[pl.BlockSpec((None, 4, DIFF_HEAD_DIM), lambda b, g, pt: (l, 0, 0)),
                  pl.BlockSpec((None, 1, 128), lambda b, g, pt: (l, 0, 0)),
                  per_b, per_b, per_b]
                 + _paged_specs(cache_k, l, PAGES_PER_STEP) + _paged_specs(cache_v, l, PAGES_PER_STEP),
        out_specs=per_b,
        scratch_shapes=[pltpu.VMEM((DIFF_HEADS, 8, LANES), F32), pltpu.VMEM((DIFF_HEADS, 8, LANES), F32),
                        pltpu.VMEM((DIFF_HEADS, 8, 128), F32)],
    )
    return pl.pallas_call(
        functools.partial(_sdiff_body, n_pages=PAGES_PER_STEP, n_new=n_new, lam_init=lam_init),
        grid_spec=grid_spec,
        out_shape=jax.ShapeDtypeStruct((nb, n_new, WIDTH), BF16),
        compiler_params=_params(("parallel", "arbitrary")),
        name="sample_diff_attn",
    )(page_table, W["diff_lambda"], W["g_subln"], q, k_new, v_new,
      *([cache_k] * PAGES_PER_STEP), *([cache_v] * PAGES_PER_STEP))


def _sfox_body(pt_ref, q_ref, kn_ref, vn_ref, lf_ref, cp_ref, *refs, n_pages, n_new):
    k_refs = refs[:n_pages]
    v_refs = refs[n_pages:2 * n_pages]
    o_ref, m_ref, l_ref, acc_ref = refs[2 * n_pages:]
    state_refs = (m_ref, l_ref, acc_ref)
    grp = pl.program_id(1)
    lf = lf_ref[...]
    f_rows = [lf[0:1]]
    for t in range(1, n_new):
        f_rows.append(f_rows[-1] + lf[t:t + 1])
    f_new = jnp.concatenate(f_rows + f_rows, axis=0)
    q8s = []
    for h in range(FOX_HEADS):
        qh = q_ref[:, h * FOX_HEAD_DIM:(h + 1) * FOX_HEAD_DIM].astype(F32)
        q8s.append(jnp.concatenate([qh, qh], axis=0))

    @pl.when(grp == 0)
    def _():
        for h in range(FOX_HEADS):
            lanes = slice(h * FOX_HEAD_DIM, (h + 1) * FOX_HEAD_DIM)
            bias_cols = [f_new[:, h:h + 1] - f_rows[j][:, h:h + 1] for j in range(n_new)]
            _store_state(*state_refs, h,
                         _new_keys_state(q8s[h], kn_ref[:, lanes], vn_ref[:, lanes], bias_cols, n_new))

    old = [_load_state(*state_refs, h) for h in range(FOX_HEADS)]
    new = []
    for h in range(FOX_HEADS):
        kt = jnp.concatenate([r[h] for r in k_refs], axis=1)
        vt = jnp.concatenate([r[h] for r in v_refs], axis=1)
        fk = jnp.concatenate([cp_ref[i, h:h + 1, :] for i in range(n_pages)], axis=1)
        s = _dot(q8s[h], kt) + (f_new[:, h:h + 1] - fk)
        new.append(_online_step(old[h], s, lambda p, vt=vt: _dot_nt(p, vt)))
    for h in range(FOX_HEADS):
        _store_state(*state_refs, h, new[h])

    @pl.when(grp == pl.num_programs(1) - 1)
    def _():
        for h in range(FOX_HEADS):
            o = acc_ref[h] / l_ref[h][:, 0:FOX_HEAD_DIM]
            o_ref[:, h * FOX_HEAD_DIM:(h + 1) * FOX_HEAD_DIM] = o[0:n_new].astype(BF16)


def _sample_fox(page_table, q, k_new, v_new, lf_new, c_past, cache_k, cache_v, l):
    nb, n_new, _ = q.shape
    n_groups = page_table.shape[1] // PAGES_PER_STEP
    page = c_past.shape[-1]
    per_b = pl.BlockSpec((None, n_new, WIDTH), lambda b, g, pt: (b, 0, 0))
    grid_spec = pltpu.PrefetchScalarGridSpec(
        num_scalar_prefetch=1,
        grid=(nb, n_groups),
        in_specs=[per_b, per_b, per_b,
                  pl.BlockSpec((None, n_new, FOX_HEADS), lambda b, g, pt: (b, 0, 0)),
                  pl.BlockSpec((None, PAGES_PER_STEP, FOX_HEADS, page), lambda b, g, pt: (b, g, 0, 0))]
                 + _paged_specs(cache_k, l, PAGES_PER_STEP) + _paged_specs(cache_v, l, PAGES_PER_STEP),
        out_specs=per_b,
        scratch_shapes=[pltpu.VMEM((FOX_HEADS, 8, LANES), F32), pltpu.VMEM((FOX_HEADS, 8, LANES), F32),
                        pltpu.VMEM((FOX_HEADS, 8, FOX_HEAD_DIM), F32)],
    )
    return pl.pallas_call(
        functools.partial(_sfox_body, n_pages=PAGES_PER_STEP, n_new=n_new),
        grid_spec=grid_spec,
        out_shape=jax.ShapeDtypeStruct((nb, n_new, WIDTH), BF16),
        compiler_params=_params(("parallel", "arbitrary")),
        name="sample_fox_attn",
    )(page_table, q, k_new, v_new, lf_new, c_past,
      *([cache_k] * PAGES_PER_STEP), *([cache_v] * PAGES_PER_STEP))


def _past_forget_body(pt_ref, lf_ref, o_ref, *, n_pages):
    b = pl.program_id(0)
    cums = [_lane_cumsum(lf_ref[pt_ref[b, p]]) for p in range(n_pages)]
    run = jnp.zeros((cums[0].shape[0], 1), F32)
    offsets = []
    for c in cums:
        offsets.append(run)
        run = run + c[:, -1:]
    for p in range(n_pages):
        o_ref[p] = (cums[p] + offsets[p]) - run


def _past_forget(page_table, logf_rows, l):
    nb, n_pages = page_table.shape
    _, n_pool, heads, page = logf_rows.shape
    grid_spec = pltpu.PrefetchScalarGridSpec(
        num_scalar_prefetch=1,
        grid=(nb,),
        in_specs=[pl.BlockSpec((None, n_pool, heads, page), lambda b, pt: (l, 0, 0, 0),
                               pipeline_mode=pl.Buffered(1))],
        out_specs=pl.BlockSpec((None, n_pages, heads, page), lambda b, pt: (b, 0, 0, 0)),
    )
    return pl.pallas_call(
        functools.partial(_past_forget_body, n_pages=n_pages),
        grid_spec=grid_spec,
        out_shape=jax.ShapeDtypeStruct((nb, n_pages, heads, page), F32),
        compiler_params=_params(("arbitrary",)),
        name="past_forget",
    )(page_table, logf_rows)


def _rope_tables(pos):
    half = DIFF_HEAD_DIM // 2
    inv = ROPE_THETA ** (-jnp.arange(half, dtype=F32) / half)
    ang = pos.astype(F32)[:, None] * inv[None, :]
    cos = jnp.cos(ang)
    sin = jnp.sin(ang)
    reps = WIDTH // DIFF_HEAD_DIM
    return (jnp.tile(jnp.concatenate([cos, cos], axis=-1), (1, reps)),
            jnp.tile(jnp.concatenate([-sin, sin], axis=-1), (1, reps)))


def _block_diag(w):
    depth, nblk, d, e = w.shape
    eye = jnp.eye(nblk, dtype=w.dtype)
    return jnp.einsum("lnde,nm->lndme", w, eye).reshape(depth, nblk * d, nblk * e)


def kernel(x_prompt, x_sample, cache_diff_k, cache_diff_v, cache_fox_k, cache_fox_v, cache_fox_logf, state_lru_h, state_conv, page_table, norm_ffn1, w_ffn1_in, w_ffn1_out, norm_mix, w_in, b_forget, diff_q_norm, diff_k_norm, diff_lambda_q1, diff_lambda_k1, diff_lambda_q2, diff_lambda_k2, diff_subln, fox_q_norm, fox_k_norm, conv_w, conv_b, lru_w_a, lru_b_a, lru_w_x, lru_b_x, lru_lambda, w_diff_out, w_fox_out, w_lru_out, w_o, norm_ffn2, w_ffn2_in, w_ffn2_out):
    depth = w_in.shape[0]
    bp, n_t, _ = x_prompt.shape
    nb, n_new, _ = x_sample.shape
    page = cache_diff_k.shape[2]
    past_len = page_table.shape[1] * page

    vec = lambda a: a[:, None, :]
    per_head = lambda a: jnp.tile(a, (1, WIDTH // a.shape[-1]))[:, None, :]
    c0 = 6 * WIDTH
    c1 = c0 + FOX_HEADS
    c2 = c1 + 2 * WIDTH
    W = {
        "norm_ffn1": vec(norm_ffn1), "norm_mix": vec(norm_mix), "norm_ffn2": vec(norm_ffn2),
        "w_ffn1_in": w_ffn1_in.astype(BF16), "w_ffn1_out": w_ffn1_out.astype(BF16),
        "w_ffn2_in": w_ffn2_in.astype(BF16), "w_ffn2_out": w_ffn2_out.astype(BF16),
        "w_qkv": jnp.concatenate(
            [w_in[:, :, :c0], w_in[:, :, c1:c2],
             jnp.pad(w_in[:, :, c0:c1], ((0, 0), (0, 0), (0, LANES - FOX_HEADS)))], axis=-1).astype(BF16),
        "w_gates": w_in[:, :, c2:].astype(BF16),
        "head_sum": jnp.kron(jnp.eye(WIDTH // DIFF_HEAD_DIM, dtype=F32),
                             jnp.ones((DIFF_HEAD_DIM, DIFF_HEAD_DIM), F32)).astype(BF16),
        "g_dq": per_head(diff_q_norm), "g_dk": per_head(diff_k_norm),
        "g_fq": per_head(fox_q_norm), "g_fk": per_head(fox_k_norm),
        "g_subln": vec(diff_subln), "g_subln_col": diff_subln[:, :, None],
        "b_forget": jnp.pad(b_forget, ((0, 0), (0, LANES - FOX_HEADS)))[:, None, :],
        "diff_lambda": jnp.stack([diff_lambda_q1, diff_lambda_k1, diff_lambda_q2, diff_lambda_k2], axis=1),
        "conv_w": conv_w, "conv_b": vec(conv_b),
        "lru_w_ax": jnp.concatenate([_block_diag(lru_w_a), _block_diag(lru_w_x)], axis=-1).astype(BF16),
        "lru_b_a": vec(lru_b_a), "lru_b_x": vec(lru_b_x), "lru_lambda": vec(lru_lambda),
        "w_diff_out": w_diff_out.astype(BF16), "w_fox_out": w_fox_out.astype(BF16),
        "w_lru_out": w_lru_out.astype(BF16), "w_o": w_o.astype(BF16),
    }
    logf_rows = jnp.swapaxes(cache_fox_logf, 2, 3)
    n_pool = cache_diff_k.shape[1]
    diff_k_pages = cache_diff_k.reshape(depth, n_pool, page * DIFF_HEADS, 2 * DIFF_HEAD_DIM)
    diff_v_pages = cache_diff_v.reshape(depth, n_pool, page * DIFF_HEADS, 2 * DIFF_HEAD_DIM)
    fox_k_pages = jnp.transpose(cache_fox_k, (0, 1, 3, 4, 2))
    fox_v_pages = jnp.transpose(cache_fox_v, (0, 1, 3, 4, 2))

    cos_p, sin_p = _rope_tables(jnp.arange(n_t))
    cos_s, sin_s = _rope_tables(jnp.tile(past_len + jnp.arange(n_new), nb))

    tm_p, t_attn, tt_lru = 512, 512, 512
    m_s = nb * n_new
    yp = x_prompt.reshape(bp * n_t, D_MODEL)
    ys = x_sample.reshape(m_s, D_MODEL)
    p_rows, s_rows = [], []
    p_leaves = s_leaves = None
    for l in range(depth):
        lam_init = 0.8 - 0.6 * math.exp(-0.3 * l)

        yp = _ffn(yp, W["norm_ffn1"], W["w_ffn1_in"], W["w_ffn1_out"], l, tm_p)
        dq, dk, dkb, dv, dvb, fq, fk, fkb, fv, fvb, lf, lx, lg = _proj(
            yp, W, l, depth, tm_p, cos_p, sin_p, n_t // tm_p, p_leaves)
        p_leaves = (dk, dv, fk, fv)
        seq = lambda a: a.reshape(bp, n_t, a.shape[-1])
        d_out = _prompt_diff(seq(dq), seq(dkb), seq(dvb), W, l, lam_init, t_attn)
        f_row = _prompt_cum_forget(jnp.swapaxes(seq(lf), 1, 2))
        f_out = _prompt_fox(seq(fq), seq(fkb), seq(fvb), jnp.swapaxes(f_row, 1, 2), f_row, t_attn)
        l_out, h_last = _prompt_lru(seq(lx), seq(lg), W, l, tt_lru)
        flat = lambda a: a.reshape(bp * n_t, WIDTH)
        yp = _merge(yp, flat(d_out), flat(f_out), flat(l_out), W, l, tm_p)
        yp = _ffn(yp, W["norm_ffn2"], W["w_ffn2_in"], W["w_ffn2_out"], l, tm_p)
        p_rows.append((seq(lf), h_last.reshape(bp, WIDTH), seq(lx)[:, n_t - (CONV_WIDTH - 1):, :]))

        ys = _ffn(ys, W["norm_ffn1"], W["w_ffn1_in"], W["w_ffn1_out"], l, m_s)
        dq, dk, dkb, dv, dvb, fq, fk, fkb, fv, fvb, lf, lx, lg = _proj(
            ys, W, l, depth, m_s, cos_s, sin_s, 1, s_leaves)
        s_leaves = (dk, dv, fk, fv)
        tok = lambda a: a.reshape(nb, n_new, a.shape[-1])
        d_out = _sample_diff(page_table, tok(dq), tok(dk[l]), tok(dv[l]), diff_k_pages, diff_v_pages, W, l,
                             lam_init)
        c_past = _past_forget(page_table, logf_rows, l)
        f_out = _sample_fox(page_table, tok(fq), tok(fk[l]), tok(fv[l]), tok(lf), c_past, fox_k_pages,
                            fox_v_pages, l)
        lo_t, h_last = _sample_lru(jnp.swapaxes(tok(lx), 0, 1), jnp.swapaxes(tok(lg), 0, 1),
                                   jnp.swapaxes(state_conv[l], 0, 1), state_lru_h[l], W, l)
        l_out = jnp.swapaxes(lo_t, 0, 1)
        flat = lambda a: a.reshape(m_s, WIDTH)
        ys = _merge(ys, flat(d_out), flat(f_out), flat(l_out), W, l, m_s)
        ys = _ffn(ys, W["norm_ffn2"], W["w_ffn2_in"], W["w_ffn2_out"], l, m_s)
        conv_last = jnp.concatenate([state_conv[l], tok(lx)], axis=1)[:, n_new:, :]
        s_rows.append((tok(lf), h_last, conv_last))

    def leaves(stacked, rows, b, t):
        dk, dv, fk, fv = stacked
        return (dk.reshape(depth, b, t, DIFF_HEADS, 2 * DIFF_HEAD_DIM),
                dv.reshape(depth, b, t, DIFF_HEADS, 2 * DIFF_HEAD_DIM),
                fk.reshape(depth, b, t, FOX_HEADS, FOX_HEAD_DIM),
                fv.reshape(depth, b, t, FOX_HEADS, FOX_HEAD_DIM)) + tuple(
                    jnp.stack([r[i] for r in rows]) for i in range(3))

    return ((yp.reshape(bp, n_t, D_MODEL), ys.reshape(nb, n_new, D_MODEL))
            + leaves(p_leaves, p_rows, bp, n_t) + leaves(s_leaves, s_rows, nb, n_new))
```

```python
import functools
import math

import jax
import jax.numpy as jnp
from jax import lax
from jax.experimental import pallas as pl
from jax.experimental.pallas import tpu as pltpu

F32 = jnp.float32
BF16 = jnp.bfloat16

D_MODEL = 1024
DIFF_HEADS = 4
DIFF_HEAD_DIM = 64
FOX_HEADS = 8
FOX_HEAD_DIM = 64
WIDTH = 512
HEAD_GROUPS = WIDTH // 128
LRU_C = 8.0
CONV_WIDTH = 4
D_FF = 2816
N_BRANCH = 3
ROPE_THETA = 10000.0
EPS = 1e-6
LOG2E = math.log2(math.e)
QKV_COLS = 8 * WIDTH + 128

V7X_VMEM_LIMIT = 56 * 1024 * 1024
LANES = 128


def _params(semantics):
    return pltpu.CompilerParams(dimension_semantics=semantics, vmem_limit_bytes=V7X_VMEM_LIMIT)


def _resident(shape, index_map):
    return pl.BlockSpec(shape, index_map, pipeline_mode=pl.Buffered(1))


def _rms(x, g):
    return x * lax.rsqrt(jnp.mean(x * x, axis=-1, keepdims=True) + EPS) * g


def _dot(a, b):
    return jnp.dot(a, b, preferred_element_type=F32)


def _dot_nt(a, b):
    return lax.dot_general(a, b, (((1,), (1,)), ((), ())), preferred_element_type=F32)


def _softplus(x):
    return jnp.maximum(x, 0.0) + jnp.log1p(jnp.exp(-jnp.abs(x)))


def _gelu_tanh(x):
    return 0.5 * x * (1.0 + jnp.tanh(math.sqrt(2.0 / math.pi) * (x + 0.044715 * (x * x * x))))


def _ffn_body(x_ref, g_ref, wg_ref, wu_ref, wo_ref, o_ref):
    x = x_ref[...]
    xb = _rms(x, g_ref[...]).astype(BF16)
    g = _dot(xb, wg_ref[...])
    u = _dot(xb, wu_ref[...])
    act = (g * jax.nn.sigmoid(g) * u).astype(BF16)
    o_ref[...] = x + 0.5 * _dot(act, wo_ref[...])


def _ffn(x, gain, w_in, w_out, l, tm):
    m = x.shape[0]
    return pl.pallas_call(
        _ffn_body,
        grid=(m // tm,),
        in_specs=[
            pl.BlockSpec((tm, D_MODEL), lambda i: (i, 0)),
            _resident((None, 1, D_MODEL), lambda i: (l, 0, 0)),
            _resident((None, D_MODEL, D_FF), lambda i: (l, 0, 0)),
            _resident((None, D_MODEL, D_FF), lambda i: (l, 0, 1)),
            _resident((None, D_FF, D_MODEL), lambda i: (l, 0, 0)),
        ],
        out_specs=pl.BlockSpec((tm, D_MODEL), lambda i: (i, 0)),
        out_shape=jax.ShapeDtypeStruct((m, D_MODEL), F32),
        compiler_params=_params(("parallel",)),
        name="ffn",
    )(x, gain, w_in, w_in, w_out)


N_PROJ_OUT = 13
PROJ_LEAF_OUT = (1, 3, 6, 8)


def _proj_body(x_ref, g_ref, w_ref, bd_ref, cos_ref, sin_ref, gdq_ref, gdk_ref, gfq_ref, gfk_ref, bf_ref, *refs):
    dq_o, dk_o, dkb_o, dv_o, dvb_o, fq_o, fk_o, fkb_o, fv_o, fvb_o, lf_o, lx_o, lg_o = refs[-N_PROJ_OUT:]
    tm = x_ref.shape[0]
    xb = _rms(x_ref[...], g_ref[...]).astype(BF16)
    y = _dot(xb, w_ref[...])
    bd = bd_ref[...]

    def head_rms(t, g):
        ss = _dot((t * t).astype(BF16), bd)
        return t * lax.rsqrt(ss * (1.0 / DIFF_HEAD_DIM) + EPS) * g

    lane = lax.broadcasted_iota(jnp.int32, (tm, WIDTH), 1)
    first_half = (lane & (DIFF_HEAD_DIM // 2)) == 0
    cos = cos_ref[...]
    sin = sin_ref[...]

    def rope(t):
        partner = jnp.where(first_half, pltpu.roll(t, WIDTH - DIFF_HEAD_DIM // 2, 1),
                            pltpu.roll(t, DIFF_HEAD_DIM // 2, 1))
        return t * cos + partner * sin

    scale = LOG2E * DIFF_HEAD_DIM ** -0.5
    dq = rope(head_rms(y[:, 0 * WIDTH:1 * WIDTH], gdq_ref[...]))
    dk = rope(head_rms(y[:, 1 * WIDTH:2 * WIDTH], gdk_ref[...]))
    dv = y[:, 2 * WIDTH:3 * WIDTH]
    fq = head_rms(y[:, 3 * WIDTH:4 * WIDTH], gfq_ref[...])
    fk = head_rms(y[:, 4 * WIDTH:5 * WIDTH], gfk_ref[...])
    fv = y[:, 5 * WIDTH:6 * WIDTH]
    z = y[:, 8 * WIDTH:8 * WIDTH + LANES] + bf_ref[...]
    lf = jnp.minimum(z, 0.0) - jnp.log1p(jnp.exp(-jnp.abs(z)))

    dq_o[...] = (dq * scale).astype(BF16)
    dk_o[...] = dk
    dkb_o[...] = dk.astype(BF16)
    dv_o[...] = dv
    dvb_o[...] = dv.astype(BF16)
    fq_o[...] = (fq * scale).astype(BF16)
    fk_o[...] = fk
    fkb_o[...] = fk.astype(BF16)
    fv_o[...] = fv
    fvb_o[...] = fv.astype(BF16)
    lf_o[...] = lf[:, :FOX_HEADS]
    lx_o[...] = y[:, 6 * WIDTH:7 * WIDTH]
    lg_o[...] = y[:, 7 * WIDTH:8 * WIDTH]


def _proj(x, W, l, depth, tm, cos, sin, n_pos_blocks, leaves):
    m = x.shape[0]
    row = lambda width: pl.BlockSpec((tm, width), lambda i: (i, 0))
    vec = lambda width: _resident((None, 1, width), lambda i: (l, 0, 0))
    pos = pl.BlockSpec((tm, WIDTH), lambda i: (i % n_pos_blocks, 0))
    slot = pl.BlockSpec((None, tm, WIDTH), lambda i: (l, i, 0))
    f32w = jax.ShapeDtypeStruct((m, WIDTH), F32)
    b16w = jax.ShapeDtypeStruct((m, WIDTH), BF16)
    stacked = jax.ShapeDtypeStruct((depth, m, WIDTH), F32)
    out_specs = [row(WIDTH)] * 10 + [row(FOX_HEADS), row(WIDTH), row(WIDTH)]
    out_shape = [b16w, f32w, b16w, f32w, b16w, b16w, f32w, b16w, f32w, b16w,
                 jax.ShapeDtypeStruct((m, FOX_HEADS), F32), f32w, f32w]
    for o in PROJ_LEAF_OUT:
        out_specs[o] = slot
        out_shape[o] = stacked
    in_specs = [
        row(D_MODEL), vec(D_MODEL),
        _resident((None, D_MODEL, QKV_COLS), lambda i: (l, 0, 0)),
        _resident((WIDTH, WIDTH), lambda i: (0, 0)),
        pos, pos, vec(WIDTH), vec(WIDTH), vec(WIDTH), vec(WIDTH), vec(LANES),
    ]
    args = [x, W["norm_mix"], W["w_qkv"], W["head_sum"], cos, sin,
            W["g_dq"], W["g_dk"], W["g_fq"], W["g_fk"], W["b_forget"]]
    aliases = {}
    if leaves is not None:
        aliases = {len(args) + n: o for n, o in enumerate(PROJ_LEAF_OUT)}
        in_specs += [pl.BlockSpec(memory_space=pl.ANY)] * len(leaves)
        args += list(leaves)
    return pl.pallas_call(
        _proj_body,
        grid=(m // tm,),
        in_specs=in_specs,
        out_specs=out_specs,
        out_shape=out_shape,
        input_output_aliases=aliases,
        compiler_params=_params(("parallel",)),
        name="proj",
    )(*args)


def _merge_body(x_ref, g_ref, wg_ref, d_ref, f_ref, r_ref, wd_ref, wf_ref, wl_ref, wo_ref, o_ref):
    x = x_ref[...]
    xb = _rms(x, g_ref[...]).astype(BF16)
    gates = jax.nn.sigmoid(_dot(xb, wg_ref[...]))
    merged = (gates[:, 0:D_MODEL] * _dot(d_ref[...], wd_ref[...])
              + gates[:, D_MODEL:2 * D_MODEL] * _dot(f_ref[...], wf_ref[...])
              + gates[:, 2 * D_MODEL:] * _dot(r_ref[...], wl_ref[...]))
    o_ref[...] = x + _dot(merged.astype(BF16), wo_ref[...])


def _merge(x, d_out, f_out, l_out, W, l, tm):
    m = x.shape[0]
    row = lambda width: pl.BlockSpec((tm, width), lambda i: (i, 0))
    mat = lambda r, c: _resident((None, r, c), lambda i: (l, 0, 0))
    return pl.pallas_call(
        _merge_body,
        grid=(m // tm,),
        in_specs=[row(D_MODEL), mat(1, D_MODEL), mat(D_MODEL, N_BRANCH * D_MODEL),
                  row(WIDTH), row(WIDTH), row(WIDTH),
                  mat(WIDTH, D_MODEL), mat(WIDTH, D_MODEL), mat(WIDTH, D_MODEL), mat(D_MODEL, D_MODEL)],
        out_specs=row(D_MODEL),
        out_shape=jax.ShapeDtypeStruct((m, D_MODEL), F32),
        compiler_params=_params(("parallel",)),
        name="merge",
    )(x, W["norm_mix"], W["w_gates"], d_out, f_out, l_out,
      W["w_diff_out"], W["w_fox_out"], W["w_lru_out"], W["w_o"])


def _stack_halves(qg):
    lane = lax.broadcasted_iota(jnp.int32, qg.shape, 1)
    zero = jnp.zeros_like(qg)
    return jnp.concatenate([jnp.where(lane < 64, qg, zero), jnp.where(lane >= 64, qg, zero)], axis=0)


def _flash_groups(qqs, k_ref, vt_ref, n_full, t, bias_fns):
    cols = 2 * t
    key = lax.broadcasted_iota(jnp.int32, (t, cols), 0)
    qry = lax.broadcasted_iota(jnp.int32, (t, cols), 1)
    causal = key <= jnp.where(qry >= t, qry - t, qry)

    def step(j, carry, masked):
        start = pl.multiple_of(j * t, t)
        out = []
        for grp, (m, l, acc) in enumerate(carry):
            lanes = slice(grp * 128, (grp + 1) * 128)
            s = _dot_nt(k_ref[pl.ds(start, t), lanes], qqs[grp])
            if bias_fns is not None:
                s = bias_fns[grp](s, start)
            if masked:
                s = jnp.where(causal, s, -jnp.inf)
            m_new = jnp.maximum(m, jnp.max(s, axis=0, keepdims=True))
            alpha = jnp.exp2(m - m_new)
            p = jnp.exp2(s - m_new)
            l = alpha * l + jnp.sum(p, axis=0, keepdims=True)
            acc = alpha * acc + _dot(vt_ref[j, lanes, :], p.astype(BF16))
            out.append((m_new, l, acc))
        return tuple(out)

    init = tuple((jnp.full((1, cols), -jnp.inf, F32), jnp.zeros((1, cols), F32), jnp.zeros((128, cols), F32))
                 for _ in qqs)
    carry = lax.fori_loop(0, n_full, lambda j, c: step(j, c, False), init)
    return [(l, acc) for _, l, acc in step(n_full, carry, True)]


def _diff_lambda(lv_ref, lam_init):
    lv = lv_ref[...]
    s1 = jnp.sum(lv[0:1] * lv[1:2], axis=-1, keepdims=True)
    s2 = jnp.sum(lv[2:3] * lv[3:4], axis=-1, keepdims=True)
    return jnp.exp(s1) - jnp.exp(s2) + lam_init


def _pdiff_body(lv_ref, g_ref, q_ref, k_ref, vt_ref, o_ref, *, t, lam_init):
    i = pl.program_id(1)
    lam = _diff_lambda(lv_ref, lam_init)
    qqs = [_stack_halves(q_ref[:, h * 128:(h + 1) * 128]) for h in range(DIFF_HEADS)]
    for h, (l, acc) in enumerate(_flash_groups(qqs, k_ref, vt_ref, i, t, None)):
        o = acc[:, :t] / l[:, :t] - lam * (acc[:, t:] / l[:, t:])
        o = o * lax.rsqrt(jnp.mean(o * o, axis=0, keepdims=True) + EPS) * g_ref[...]
        o_ref[:, h * 128:(h + 1) * 128] = (o * (1.0 - lam_init)).T.astype(BF16)


def _kv_tiles_t(v, t):
    b, n_t, w = v.shape
    return jnp.swapaxes(v.reshape(b, n_t // t, t, w), 2, 3)


def _prompt_diff(q, k, v, W, l, lam_init, t):
    b, n_t, _ = q.shape
    seq = pl.BlockSpec((None, n_t, WIDTH), lambda bi, i: (bi, 0, 0))
    seq_t = pl.BlockSpec((None, n_t // t, WIDTH, t), lambda bi, i: (bi, 0, 0, 0))
    tile = pl.BlockSpec((None, t, WIDTH), lambda bi, i: (bi, i, 0))
    return pl.pallas_call(
        functools.partial(_pdiff_body, t=t, lam_init=lam_init),
        grid=(b, n_t // t),
        in_specs=[pl.BlockSpec((None, 4, DIFF_HEAD_DIM), lambda bi, i: (l, 0, 0)),
                  pl.BlockSpec((None, 128, 1), lambda bi, i: (l, 0, 0)),
                  tile, seq, seq_t],
        out_specs=tile,
        out_shape=jax.ShapeDtypeStruct((b, n_t, WIDTH), BF16),
        compiler_params=_params(("parallel", "arbitrary")),
        name="prompt_diff_attn",
    )(W["diff_lambda"], W["g_subln_col"], q, k, _kv_tiles_t(v, t))


def _pfox_body(q_ref, k_ref, vt_ref, fq_ref, fk_ref, o_ref, *, t):
    i = pl.program_id(1)
    dim = lax.broadcasted_iota(jnp.int32, (128, t), 0)
    qqs = [_stack_halves(q_ref[:, grp * 128:(grp + 1) * 128]) for grp in range(HEAD_GROUPS)]

    def make_bias(grp):
        fq0 = LOG2E * fq_ref[2 * grp, pl.ds(i, 1), :]
        fq1 = LOG2E * fq_ref[2 * grp + 1, pl.ds(i, 1), :]

        def bias(s, start):
            fk0 = LOG2E * fk_ref[pl.ds(start, t), 2 * grp:2 * grp + 1]
            fk1 = LOG2E * fk_ref[pl.ds(start, t), 2 * grp + 1:2 * grp + 2]
            return jnp.concatenate([s[:, :t] + (fq0 - fk0), s[:, t:] + (fq1 - fk1)], axis=1)

        return bias

    biases = [make_bias(grp) for grp in range(HEAD_GROUPS)]
    for grp, (l, acc) in enumerate(_flash_groups(qqs, k_ref, vt_ref, i, t, biases)):
        o = jnp.where(dim < 64, acc[:, :t] / l[:, :t], acc[:, t:] / l[:, t:])
        o_ref[:, grp * 128:(grp + 1) * 128] = o.T.astype(BF16)


def _prompt_fox(q, k, v, f_col, f_row, t):
    b, n_t, _ = q.shape
    seq = pl.BlockSpec((None, n_t, WIDTH), lambda bi, i: (bi, 0, 0))
    seq_t = pl.BlockSpec((None, n_t // t, WIDTH, t), lambda bi, i: (bi, 0, 0, 0))
    tile = pl.BlockSpec((None, t, WIDTH), lambda bi, i: (bi, i, 0))
    return pl.pallas_call(
        functools.partial(_pfox_body, t=t),
        grid=(b, n_t // t),
        in_specs=[tile, seq, seq_t,
                  pl.BlockSpec((None, FOX_HEADS, n_t // t, t), lambda bi, i: (bi, 0, 0, 0)),
                  pl.BlockSpec((None, n_t, FOX_HEADS), lambda bi, i: (bi, 0, 0))],
        out_specs=tile,
        out_shape=jax.ShapeDtypeStruct((b, n_t, WIDTH), BF16),
        compiler_params=_params(("parallel", "arbitrary")),
        name="prompt_fox_attn",
    )(q, k, _kv_tiles_t(v, t), f_row.reshape(b, FOX_HEADS, n_t // t, t), f_col)


def _lane_cumsum(x):
    n = x.shape[-1]
    lane = lax.broadcasted_iota(jnp.int32, x.shape, x.ndim - 1)
    s = 1
    while s < n:
        x = x + jnp.where(lane >= s, pltpu.roll(x, s, x.ndim - 1), 0.0)
        s *= 2
    return x


def _cumsum_body(x_ref, o_ref):
    o_ref[...] = _lane_cumsum(x_ref[...])


def _prompt_cum_forget(lf_rows):
    b, h, n_t = lf_rows.shape
    spec = pl.BlockSpec((None, h, n_t), lambda bi: (bi, 0, 0))
    return pl.pallas_call(
        _cumsum_body, grid=(b,), in_specs=[spec], out_specs=spec,
        out_shape=jax.ShapeDtypeStruct((b, h, n_t), F32),
        compiler_params=_params(("parallel",)), name="cum_forget",
    )(lf_rows)


def _lru_gates(xc, wax_ref, ba_ref, bx_ref, lam_ref):
    ga = _dot(xc.astype(BF16), wax_ref[...])
    r = jax.nn.sigmoid(ga[:, :WIDTH] + ba_ref[...])
    i = jax.nn.sigmoid(ga[:, WIDTH:] + bx_ref[...])
    log_a = -LRU_C * r * _softplus(-lam_ref[...])
    a = jnp.exp(log_a)
    u = jnp.sqrt(-jnp.tanh(log_a) * (a * a + 1.0)) * (i * xc)
    return a, u


def _plru_body(lx_ref, lg_ref, cw_ref, cb_ref, wax_ref, ba_ref, bx_ref, lam_ref, lo_ref, hl_ref,
               xs_ref, h_ref, *, tt):
    @pl.when(pl.program_id(1) == 0)
    def _():
        xs_ref[0:8, :] = jnp.zeros((8, WIDTH), F32)
        h_ref[...] = jnp.zeros((8, WIDTH), F32)

    x = lx_ref[...]
    xs_ref[8:8 + tt, :] = x
    cw = cw_ref[...]
    xc = cb_ref[...] + xs_ref[5:5 + tt, :] * cw[0:1]
    xc = xc + xs_ref[6:6 + tt, :] * cw[1:2]
    xc = xc + xs_ref[7:7 + tt, :] * cw[2:3]
    xc = xc + x * cw[3:4]
    xs_ref[0:8, :] = x[tt - 8:tt]

    a, u = _lru_gates(xc, wax_ref, ba_ref, bx_ref, lam_ref)
    row = lax.broadcasted_iota(jnp.int32, (tt, WIDTH), 0)
    s = 1
    while s < tt:
        keep = row >= s
        u = a * jnp.where(keep, pltpu.roll(u, s, 0), 0.0) + u
        a = a * jnp.where(keep, pltpu.roll(a, s, 0), 1.0)
        s *= 2
    hs = a * h_ref[0:1, :] + u
    h_ref[0:1, :] = hs[tt - 1:tt]
    hl_ref[...] = hs[tt - 1:tt]
    lo_ref[...] = (hs * _gelu_tanh(lg_ref[...])).astype(BF16)


def _prompt_lru(lx, lg, W, l, tt):
    b, n_t, _ = lx.shape
    tile = pl.BlockSpec((None, tt, WIDTH), lambda bi, i: (bi, i, 0))
    mat = lambda r, c: _resident((None, r, c), lambda bi, i: (l, 0, 0))
    return pl.pallas_call(
        functools.partial(_plru_body, tt=tt),
        grid=(b, n_t // tt),
        in_specs=[tile, tile, mat(CONV_WIDTH, WIDTH), mat(1, WIDTH), mat(WIDTH, 2 * WIDTH),
                  mat(1, WIDTH), mat(1, WIDTH), mat(1, WIDTH)],
        out_specs=[tile, pl.BlockSpec((None, 1, WIDTH), lambda bi, i: (bi, 0, 0))],
        out_shape=[jax.ShapeDtypeStruct((b, n_t, WIDTH), BF16), jax.ShapeDtypeStruct((b, 1, WIDTH), F32)],
        scratch_shapes=[pltpu.VMEM((tt + 8, WIDTH), F32), pltpu.VMEM((8, WIDTH), F32)],
        compiler_params=_params(("parallel", "arbitrary")),
        name="prompt_lru",
    )(lx, lg, W["conv_w"], W["conv_b"], W["lru_w_ax"], W["lru_b_a"], W["lru_b_x"], W["lru_lambda"])


def _slru_body(lx_ref, lg_ref, c0_ref, h0_ref, cw_ref, cb_ref, wax_ref, ba_ref, bx_ref, lam_ref,
               lo_ref, hl_ref, *, n_t, nb):
    cw = cw_ref[...]
    xs = [c0_ref[j] for j in range(CONV_WIDTH - 1)] + [lx_ref[t] for t in range(n_t)]
    xcs = []
    for t in range(n_t):
        xc = cb_ref[...] + xs[t] * cw[0:1]
        for j in range(1, CONV_WIDTH):
            xc = xc + xs[t + j] * cw[j:j + 1]
        xcs.append(xc)
    a, u = _lru_gates(jnp.concatenate(xcs, axis=0), wax_ref, ba_ref, bx_ref, lam_ref)
    h = h0_ref[...]
    for t in range(n_t):
        h = a[t * nb:(t + 1) * nb] * h + u[t * nb:(t + 1) * nb]
        lo_ref[t] = (h * _gelu_tanh(lg_ref[t])).astype(BF16)
    hl_ref[...] = h


def _sample_lru(lx_t, lg_t, conv0_t, h0, W, l):
    n_t, nb, _ = lx_t.shape
    full = lambda *shape: pl.BlockSpec(shape, lambda i: (0,) * len(shape))
    mat = lambda r, c: pl.BlockSpec((None, r, c), lambda i: (l, 0, 0))
    return pl.pallas_call(
        functools.partial(_slru_body, n_t=n_t, nb=nb),
        grid=(1,),
        in_specs=[full(n_t, nb, WIDTH), full(n_t, nb, WIDTH), full(CONV_WIDTH - 1, nb, WIDTH), full(nb, WIDTH),
                  mat(CONV_WIDTH, WIDTH), mat(1, WIDTH), mat(WIDTH, 2 * WIDTH),
                  mat(1, WIDTH), mat(1, WIDTH), mat(1, WIDTH)],
        out_specs=[full(n_t, nb, WIDTH), full(nb, WIDTH)],
        out_shape=[jax.ShapeDtypeStruct((n_t, nb, WIDTH), BF16), jax.ShapeDtypeStruct((nb, WIDTH), F32)],
        compiler_params=_params(("arbitrary",)),
        name="sample_lru",
    )(lx_t, lg_t, conv0_t, h0, W["conv_w"], W["conv_b"], W["lru_w_ax"], W["lru_b_a"], W["lru_b_x"],
      W["lru_lambda"])


PAGES_PER_STEP = 32


def _online_step(state, s, pv):
    m_old, l_old, acc_old = state
    m_new = jnp.maximum(m_old, jnp.max(s, axis=-1, keepdims=True))
    alpha = jnp.exp2(m_old - m_new)
    p = jnp.exp2(s - m_new)
    return m_new, alpha * l_old + jnp.sum(p, axis=-1, keepdims=True), alpha * acc_old + pv(p)


def _load_state(m_ref, l_ref, acc_ref, h):
    return m_ref[h][:, 0:1], l_ref[h][:, 0:1], acc_ref[h]


def _store_state(m_ref, l_ref, acc_ref, h, state):
    m, l, acc = state
    m_ref[h] = jnp.broadcast_to(m, (8, LANES))
    l_ref[h] = jnp.broadcast_to(l, (8, LANES))
    acc_ref[h] = acc


def _new_keys_state(q8, kn, vn, bias_cols, n_new):
    tok = lax.broadcasted_iota(jnp.int32, (8, 1), 0) % n_new
    cols = []
    for j in range(n_new):
        sj = jnp.sum(q8 * kn[j:j + 1, :], axis=-1, keepdims=True)
        if bias_cols is not None:
            sj = sj + bias_cols[j]
        cols.append(jnp.where(tok >= j, sj, -jnp.inf))
    m = cols[0]
    for sj in cols[1:]:
        m = jnp.maximum(m, sj)
    ps = [jnp.exp2(sj - m) for sj in cols]
    l = ps[0]
    acc = ps[0] * vn[0:1, :]
    for j in range(1, n_new):
        l = l + ps[j]
        acc = acc + ps[j] * vn[j:j + 1, :]
    return m, l, acc


def _sdiff_body(pt_ref, lv_ref, g_ref, q_ref, kn_ref, vn_ref, *refs, n_pages, n_new, lam_init):
    k_refs = refs[:n_pages]
    v_refs = refs[n_pages:2 * n_pages]
    o_ref, m_ref, l_ref, acc_ref = refs[2 * n_pages:]
    state_refs = (m_ref, l_ref, acc_ref)
    grp = pl.program_id(1)
    lane = lax.broadcasted_iota(jnp.int32, (8, 128), 1)
    row = lax.broadcasted_iota(jnp.int32, (8, 128), 0)
    q8s = []
    for h in range(DIFF_HEADS):
        qh = q_ref[:, h * 128:(h + 1) * 128].astype(F32)
        q8 = jnp.concatenate([qh, qh], axis=0)
        q8s.append(jnp.where((lane < 64) == (row < n_new), q8, 0.0))

    @pl.when(grp == 0)
    def _():
        for h in range(DIFF_HEADS):
            lanes = slice(h * 128, (h + 1) * 128)
            _store_state(*state_refs, h,
                         _new_keys_state(q8s[h], kn_ref[:, lanes], vn_ref[:, lanes], None, n_new))

    page = k_refs[0].shape[0] // DIFF_HEADS
    old = [_load_state(*state_refs, h) for h in range(DIFF_HEADS)]
    new = []
    for h in range(DIFF_HEADS):
        head_rows = pl.ds(h, page, stride=DIFF_HEADS)
        kcat = jnp.concatenate([r[head_rows, :] for r in k_refs], axis=0)
        vcat = jnp.concatenate([r[head_rows, :] for r in v_refs], axis=0)
        new.append(_online_step(old[h], _dot_nt(q8s[h], kcat), lambda p, vcat=vcat: _dot(p, vcat)))
    for h in range(DIFF_HEADS):
        _store_state(*state_refs, h, new[h])

    @pl.when(grp == pl.num_programs(1) - 1)
    def _():
        lam = _diff_lambda(lv_ref, lam_init)
        for h in range(DIFF_HEADS):
            o = acc_ref[h] / l_ref[h]
            o = o[0:n_new] - lam * o[n_new:2 * n_new]
            o_ref[:, h * 128:(h + 1) * 128] = (_rms(o, g_ref[...]) * (1.0 - lam_init)).astype(BF16)


def _paged_specs(cache, l, n_pages):
    tail = cache.shape[2:]
    zeros = (0,) * len(tail)
    return [pl.BlockSpec((None, None) + tail,
                         lambda b, g, pt, i=i: (l, pt[b, g * n_pages + i]) + zeros)
            for i in range(n_pages)]


def _sample_diff(page_table, q, k_new, v_new, cache_k, cache_v, W, l, lam_init):
    nb, n_new, _ = q.shape
    n_groups = page_table.shape[1] // PAGES_PER_STEP
    per_b = pl.BlockSpec((None, n_new, WIDTH), lambda b, g, pt: (b, 0, 0))
    grid_spec = pltpu.PrefetchScalarGridSpec(
        num_scalar_prefetch=1,
        grid=(nb, n_groups),
        in_specs=[pl.BlockSpec((None, 4, DIFF_HEAD_DIM), lambda b, g, pt: (l, 0, 0)),
                  pl.BlockSpec((None, 1, 128), lambda b, g, pt: (l, 0, 0)),
                  per_b, per_b, per_b]
                 + _paged_specs(cache_k, l, PAGES_PER_STEP) + _paged_specs(cache_v, l, PAGES_PER_STEP),
        out_specs=per_b,
        scratch_shapes=[pltpu.VMEM((DIFF_HEADS, 8, LANES), F32), pltpu.VMEM((DIFF_HEADS, 8, LANES), F32),
                        pltpu.VMEM((DIFF_HEADS, 8, 128), F32)],
    )
    return pl.pallas_call(
        functools.partial(_sdiff_body, n_pages=PAGES_PER_STEP, n_new=n_new, lam_init=lam_init),
        grid_spec=grid_spec,
        out_shape=jax.ShapeDtypeStruct((nb, n_new, WIDTH), BF16),
        compiler_params=_params(("parallel", "arbitrary")),
        name="sample_diff_attn",
    )(page_table, W["diff_lambda"], W["g_subln"], q, k_new, v_new,
      *([cache_k] * PAGES_PER_STEP), *([cache_v] * PAGES_PER_STEP))


def _sfox_body(pt_ref, q_ref, kn_ref, vn_ref, lf_ref, cp_ref, *refs, n_pages, n_new):
    k_refs = refs[:n_pages]
    v_refs = refs[n_pages:2 * n_pages]
    o_ref, m_ref, l_ref, acc_ref = refs[2 * n_pages:]
    state_refs = (m_ref, l_ref, acc_ref)
    grp = pl.program_id(1)
    lf = lf_ref[...]
    f_rows = [lf[0:1]]
    for t in range(1, n_new):
        f_rows.append(f_rows[-1] + lf[t:t + 1])
    f_new = jnp.concatenate(f_rows + f_rows, axis=0)
    q8s = []
    for h in range(FOX_HEADS):
        qh = q_ref[:, h * FOX_HEAD_DIM:(h + 1) * FOX_HEAD_DIM].astype(F32)
        q8s.append(jnp.concatenate([qh, qh], axis=0))

    @pl.when(grp == 0)
    def _():
        for h in range(FOX_HEADS):
            lanes = slice(h * FOX_HEAD_DIM, (h + 1) * FOX_HEAD_DIM)
            bias_cols = [LOG2E * (f_new[:, h:h + 1] - f_rows[j][:, h:h + 1]) for j in range(n_new)]
            _store_state(*state_refs, h,
                         _new_keys_state(q8s[h], kn_ref[:, lanes], vn_ref[:, lanes], bias_cols, n_new))

    old = [_load_state(*state_refs, h) for h in range(FOX_HEADS)]
    new = []
    for h in range(FOX_HEADS):
        kt = jnp.concatenate([r[h] for r in k_refs], axis=1)
        vt = jnp.concatenate([r[h] for r in v_refs], axis=1)
        fk = jnp.concatenate([cp_ref[i, h:h + 1, :] for i in range(n_pages)], axis=1)
        s = _dot(q8s[h], kt) + LOG2E * (f_new[:, h:h + 1] - fk)
        new.append(_online_step(old[h], s, lambda p, vt=vt: _dot_nt(p, vt)))
    for h in range(FOX_HEADS):
        _store_state(*state_refs, h, new[h])

    @pl.when(grp == pl.num_programs(1) - 1)
    def _():
        for h in range(FOX_HEADS):
            o = acc_ref[h] / l_ref[h][:, 0:FOX_HEAD_DIM]
            o_ref[:, h * FOX_HEAD_DIM:(h + 1) * FOX_HEAD_DIM] = o[0:n_new].astype(BF16)


def _sample_fox(page_table, q, k_new, v_new, lf_new, c_past, cache_k, cache_v, l):
    nb, n_new, _ = q.shape
    n_groups = page_table.shape[1] // PAGES_PER_STEP
    page = c_past.shape[-1]
    per_b = pl.BlockSpec((None, n_new, WIDTH), lambda b, g, pt: (b, 0, 0))
    grid_spec = pltpu.PrefetchScalarGridSpec(
        num_scalar_prefetch=1,
        grid=(nb, n_groups),
        in_specs=[per_b, per_b, per_b,
                  pl.BlockSpec((None, n_new, FOX_HEADS), lambda b, g, pt: (b, 0, 0)),
                  pl.BlockSpec((None, PAGES_PER_STEP, FOX_HEADS, page), lambda b, g, pt: (b, g, 0, 0))]
                 + _paged_specs(cache_k, l, PAGES_PER_STEP) + _paged_specs(cache_v, l, PAGES_PER_STEP),
        out_specs=per_b,
        scratch_shapes=[pltpu.VMEM((FOX_HEADS, 8, LANES), F32), pltpu.VMEM((FOX_HEADS, 8, LANES), F32),
                        pltpu.VMEM((FOX_HEADS, 8, FOX_HEAD_DIM), F32)],
    )
    return pl.pallas_call(
        functools.partial(_sfox_body, n_pages=PAGES_PER_STEP, n_new=n_new),
        grid_spec=grid_spec,
        out_shape=jax.ShapeDtypeStruct((nb, n_new, WIDTH), BF16),
        compiler_params=_params(("parallel", "arbitrary")),
        name="sample_fox_attn",
    )(page_table, q, k_new, v_new, lf_new, c_past,
      *([cache_k] * PAGES_PER_STEP), *([cache_v] * PAGES_PER_STEP))


def _past_forget_body(pt_ref, lf_ref, o_ref, *, n_pages):
    b = pl.program_id(0)
    cums = [_lane_cumsum(lf_ref[pt_ref[b, p]]) for p in range(n_pages)]
    run = jnp.zeros((cums[0].shape[0], 1), F32)
    offsets = []
    for c in cums:
        offsets.append(run)
        run = run + c[:, -1:]
    for p in range(n_pages):
        o_ref[p] = (cums[p] + offsets[p]) - run


def _past_forget(page_table, logf_rows, l):
    nb, n_pages = page_table.shape
    _, n_pool, heads, page = logf_rows.shape
    grid_spec = pltpu.PrefetchScalarGridSpec(
        num_scalar_prefetch=1,
        grid=(nb,),
        in_specs=[pl.BlockSpec((None, n_pool, heads, page), lambda b, pt: (l, 0, 0, 0),
                               pipeline_mode=pl.Buffered(1))],
        out_specs=pl.BlockSpec((None, n_pages, heads, page), lambda b, pt: (b, 0, 0, 0)),
    )
    return pl.pallas_call(
        functools.partial(_past_forget_body, n_pages=n_pages),
        grid_spec=grid_spec,
        out_shape=jax.ShapeDtypeStruct((nb, n_pages, heads, page), F32),
        compiler_params=_params(("arbitrary",)),
        name="past_forget",
    )(page_table, logf_rows)


def _rope_tables(pos):
    half = DIFF_HEAD_DIM // 2
    inv = ROPE_THETA ** (-jnp.arange(half, dtype=F32) / half)
    ang = pos.astype(F32)[:, None] * inv[None, :]
    cos = jnp.cos(ang)
    sin = jnp.sin(ang)
    reps = WIDTH // DIFF_HEAD_DIM
    return (jnp.tile(jnp.concatenate([cos, cos], axis=-1), (1, reps)),
            jnp.tile(jnp.concatenate([-sin, sin], axis=-1), (1, reps)))


def _block_diag(w):
    depth, nblk, d, e = w.shape
    eye = jnp.eye(nblk, dtype=w.dtype)
    return jnp.einsum("lnde,nm->lndme", w, eye).reshape(depth, nblk * d, nblk * e)


def kernel(x_prompt, x_sample, cache_diff_k, cache_diff_v, cache_fox_k, cache_fox_v, cache_fox_logf, state_lru_h, state_conv, page_table, norm_ffn1, w_ffn1_in, w_ffn1_out, norm_mix, w_in, b_forget, diff_q_norm, diff_k_norm, diff_lambda_q1, diff_lambda_k1, diff_lambda_q2, diff_lambda_k2, diff_subln, fox_q_norm, fox_k_norm, conv_w, conv_b, lru_w_a, lru_b_a, lru_w_x, lru_b_x, lru_lambda, w_diff_out, w_fox_out, w_lru_out, w_o, norm_ffn2, w_ffn2_in, w_ffn2_out):
    depth = w_in.shape[0]
    bp, n_t, _ = x_prompt.shape
    nb, n_new, _ = x_sample.shape
    page = cache_diff_k.shape[2]
    past_len = page_table.shape[1] * page

    vec = lambda a: a[:, None, :]
    per_head = lambda a: jnp.tile(a, (1, WIDTH // a.shape[-1]))[:, None, :]
    c0 = 6 * WIDTH
    c1 = c0 + FOX_HEADS
    c2 = c1 + 2 * WIDTH
    W = {
        "norm_ffn1": vec(norm_ffn1), "norm_mix": vec(norm_mix), "norm_ffn2": vec(norm_ffn2),
        "w_ffn1_in": w_ffn1_in.astype(BF16), "w_ffn1_out": w_ffn1_out.astype(BF16),
        "w_ffn2_in": w_ffn2_in.astype(BF16), "w_ffn2_out": w_ffn2_out.astype(BF16),
        "w_qkv": jnp.concatenate(
            [w_in[:, :, :c0], w_in[:, :, c1:c2],
             jnp.pad(w_in[:, :, c0:c1], ((0, 0), (0, 0), (0, LANES - FOX_HEADS)))], axis=-1).astype(BF16),
        "w_gates": w_in[:, :, c2:].astype(BF16),
        "head_sum": jnp.kron(jnp.eye(WIDTH // DIFF_HEAD_DIM, dtype=F32),
                             jnp.ones((DIFF_HEAD_DIM, DIFF_HEAD_DIM), F32)).astype(BF16),
        "g_dq": per_head(diff_q_norm), "g_dk": per_head(diff_k_norm),
        "g_fq": per_head(fox_q_norm), "g_fk": per_head(fox_k_norm),
        "g_subln": vec(diff_subln), "g_subln_col": diff_subln[:, :, None],
        "b_forget": jnp.pad(b_forget, ((0, 0), (0, LANES - FOX_HEADS)))[:, None, :],
        "diff_lambda": jnp.stack([diff_lambda_q1, diff_lambda_k1, diff_lambda_q2, diff_lambda_k2], axis=1),
        "conv_w": conv_w, "conv_b": vec(conv_b),
        "lru_w_ax": jnp.concatenate([_block_diag(lru_w_a), _block_diag(lru_w_x)], axis=-1).astype(BF16),
        "lru_b_a": vec(lru_b_a), "lru_b_x": vec(lru_b_x), "lru_lambda": vec(lru_lambda),
        "w_diff_out": w_diff_out.astype(BF16), "w_fox_out": w_fox_out.astype(BF16),
        "w_lru_out": w_lru_out.astype(BF16), "w_o": w_o.astype(BF16),
    }
    logf_rows = jnp.swapaxes(cache_fox_logf, 2, 3)
    n_pool = cache_diff_k.shape[1]
    diff_k_pages = cache_diff_k.reshape(depth, n_pool, page * DIFF_HEADS, 2 * DIFF_HEAD_DIM)
    diff_v_pages = cache_diff_v.reshape(depth, n_pool, page * DIFF_HEADS, 2 * DIFF_HEAD_DIM)
    fox_k_pages = jnp.transpose(cache_fox_k, (0, 1, 3, 4, 2))
    fox_v_pages = jnp.transpose(cache_fox_v, (0, 1, 3, 4, 2))

    cos_p, sin_p = _rope_tables(jnp.arange(n_t))
    cos_s, sin_s = _rope_tables(jnp.tile(past_len + jnp.arange(n_new), nb))

    tm_p, t_attn, tt_lru = 512, 512, 512
    m_s = nb * n_new
    yp = x_prompt.reshape(bp * n_t, D_MODEL)
    ys = x_sample.reshape(m_s, D_MODEL)
    p_rows, s_rows = [], []
    p_leaves = s_leaves = None
    for l in range(depth):
        lam_init = 0.8 - 0.6 * math.exp(-0.3 * l)

        yp = _ffn(yp, W["norm_ffn1"], W["w_ffn1_in"], W["w_ffn1_out"], l, tm_p)
        dq, dk, dkb, dv, dvb, fq, fk, fkb, fv, fvb, lf, lx, lg = _proj(
            yp, W, l, depth, tm_p, cos_p, sin_p, n_t // tm_p, p_leaves)
        p_leaves = (dk, dv, fk, fv)
        seq = lambda a: a.reshape(bp, n_t, a.shape[-1])
        d_out = _prompt_diff(seq(dq), seq(dkb), seq(dvb), W, l, lam_init, t_attn)
        f_row = _prompt_cum_forget(jnp.swapaxes(seq(lf), 1, 2))
        f_out = _prompt_fox(seq(fq), seq(fkb), seq(fvb), jnp.swapaxes(f_row, 1, 2), f_row, t_attn)
        l_out, h_last = _prompt_lru(seq(lx), seq(lg), W, l, tt_lru)
        flat = lambda a: a.reshape(bp * n_t, WIDTH)
        yp = _merge(yp, flat(d_out), flat(f_out), flat(l_out), W, l, tm_p)
        yp = _ffn(yp, W["norm_ffn2"], W["w_ffn2_in"], W["w_ffn2_out"], l, tm_p)
        p_rows.append((seq(lf), h_last.reshape(bp, WIDTH), seq(lx)[:, n_t - (CONV_WIDTH - 1):, :]))

        ys = _ffn(ys, W["norm_ffn1"], W["w_ffn1_in"], W["w_ffn1_out"], l, m_s)
        dq, dk, dkb, dv, dvb, fq, fk, fkb, fv, fvb, lf, lx, lg = _proj(
            ys, W, l, depth, m_s, cos_s, sin_s, 1, s_leaves)
        s_leaves = (dk, dv, fk, fv)
        tok = lambda a: a.reshape(nb, n_new, a.shape[-1])
        d_out = _sample_diff(page_table, tok(dq), tok(dk[l]), tok(dv[l]), diff_k_pages, diff_v_pages, W, l,
                             lam_init)
        c_past = _past_forget(page_table, logf_rows, l)
        f_out = _sample_fox(page_table, tok(fq), tok(fk[l]), tok(fv[l]), tok(lf), c_past, fox_k_pages,
                            fox_v_pages, l)
        lo_t, h_last = _sample_lru(jnp.swapaxes(tok(lx), 0, 1), jnp.swapaxes(tok(lg), 0, 1),
                                   jnp.swapaxes(state_conv[l], 0, 1), state_lru_h[l], W, l)
        l_out = jnp.swapaxes(lo_t, 0, 1)
        flat = lambda a: a.reshape(m_s, WIDTH)
        ys = _merge(ys, flat(d_out), flat(f_out), flat(l_out), W, l, m_s)
        ys = _ffn(ys, W["norm_ffn2"], W["w_ffn2_in"], W["w_ffn2_out"], l, m_s)
        conv_last = jnp.concatenate([state_conv[l], tok(lx)], axis=1)[:, n_new:, :]
        s_rows.append((tok(lf), h_last, conv_last))

    def leaves(stacked, rows, b, t):
        dk, dv, fk, fv = stacked
        return (dk.reshape(depth, b, t, DIFF_HEADS, 2 * DIFF_HEAD_DIM),
                dv.reshape(depth, b, t, DIFF_HEADS, 2 * DIFF_HEAD_DIM),
                fk.reshape(depth, b, t, FOX_HEADS, FOX_HEAD_DIM),
                fv.reshape(depth, b, t, FOX_HEADS, FOX_HEAD_DIM)) + tuple(
                    jnp.stack([r[i] for r in rows]) for i in range(3))

    return ((yp.reshape(bp, n_t, D_MODEL), ys.reshape(nb, n_new, D_MODEL))
            + leaves(p_leaves, p_rows, bp, n_t) + leaves(s_leaves, s_rows, nb, n_new))
```
